```python
import jax, jax.numpy as jnp
from jax import lax
import numpy as np

D_MODEL = 1024
BATCH = 8
SEQ = 4096
DEPTH = 1

GLA_HEADS = 4
GLA_V = D_MODEL // 2
GLA_DV = GLA_V // GLA_HEADS
GLA_DK = GLA_DV // 2
GLA_QK = GLA_HEADS * GLA_DK
GLA_GATE_RANK = 16
GLA_GATE_NORMALIZER = 16.0
HGRN_HEADS = 4
HGRN_V = D_MODEL - GLA_V
HGRN_DV = HGRN_V // HGRN_HEADS
HGRN_DK = 128
HGRN_K = HGRN_HEADS * HGRN_DK
MIX_WIDTH = GLA_V + HGRN_V
CHUNK = 64
D_FF = 4 * D_MODEL
N_MOD = 6
EPS = 1e-6

IN_SPLITS = (GLA_QK, GLA_QK, GLA_V, GLA_GATE_RANK, GLA_V,
             HGRN_K, HGRN_K, HGRN_V, HGRN_V)
IN_WIDTH = sum(IN_SPLITS)
IN_OFFSETS = [int(o) for o in np.cumsum(IN_SPLITS)[:-1]]

kernel_name = "hymba_style_gla_hgrn2_adaln_layer"


def rms_norm(x, gain=None):
    xf = x.astype(jnp.float32)
    y = xf * lax.rsqrt(jnp.mean(xf * xf, axis=-1, keepdims=True) + EPS)
    if gain is not None:
        y = y * gain.astype(jnp.float32)
    return y.astype(x.dtype)


def modulate(h, shift, scale):
    return h * (1.0 + scale[:, None, :]) + shift[:, None, :]


def split_heads(t, n_heads):
    b, s, w = t.shape
    return t.reshape(b, s, n_heads, w // n_heads).transpose(0, 2, 1, 3)


def merge_heads(t):
    b, h, s, d = t.shape
    return t.transpose(0, 2, 1, 3).reshape(b, s, h * d)


def chunk_gated_linear_attention(q, k, v, log_a, chunk):
    b, h, s, dk = q.shape
    dv = v.shape[-1]
    n = s // chunk
    q = q.reshape(b, h, n, chunk, dk)
    k = k.reshape(b, h, n, chunk, dk)
    v = v.reshape(b, h, n, chunk, dv)
    cum = jnp.cumsum(log_a.reshape(b, h, n, chunk, dk), axis=3)
    cum_mid = cum[:, :, :, chunk // 2 - 1:chunk // 2, :]
    cum_last = cum[:, :, :, chunk - 1:chunk, :]
    q_rel = q * jnp.exp(cum - cum_mid)
    k_rel = k * jnp.exp(cum_mid - cum)
    causal = jnp.tril(jnp.ones((chunk, chunk), dtype=bool))
    scores = jnp.where(causal, jnp.einsum('bhncd,bhnsd->bhncs', q_rel, k_rel), 0.0)
    o_intra = jnp.einsum('bhncs,bhnse->bhnce', scores, v)
    k_to_end = k * jnp.exp(cum_last - cum)
    chunk_update = jnp.einsum('bhncd,bhnce->bhnde', k_to_end, v)
    chunk_decay = jnp.exp(cum_last[:, :, :, 0, :])

    def step(state, inp):
        decay, update = inp
        return decay[..., None] * state + update, state

    state0 = jnp.zeros((b, h, dk, dv), dtype=q.dtype)
    _, states_prev = lax.scan(step, state0,
                              (jnp.moveaxis(chunk_decay, 2, 0), jnp.moveaxis(chunk_update, 2, 0)))
    states_prev = jnp.moveaxis(states_prev, 0, 2)
    o_inter = jnp.einsum('bhncd,bhnde->bhnce', q * jnp.exp(cum), states_prev)
    return (o_intra + o_inter).reshape(b, h, s, dv)


def hybrid_mixer(h, layer, w_in, gla_gate_w2, gla_gate_b2, gla_norm_g,
                 hgrn_lb_logits, hgrn_norm_g, w_out):
    f32 = jnp.float32
    proj = h @ w_in
    gq, gk, gv, g_lr, g_gate, hq, hf, hi, h_gate = jnp.split(proj, IN_OFFSETS, axis=-1)

    gate_logits = (g_lr @ gla_gate_w2 + gla_gate_b2).astype(f32)
    log_a_gla = jax.nn.log_sigmoid(gate_logits) / GLA_GATE_NORMALIZER
    o_gla = chunk_gated_linear_attention(
        split_heads(gq.astype(f32), GLA_HEADS) * (GLA_DK ** -0.5),
        split_heads(gk.astype(f32), GLA_HEADS),
        split_heads(gv.astype(f32), GLA_HEADS),
        split_heads(log_a_gla, GLA_HEADS), CHUNK)
    o_gla = merge_heads(rms_norm(o_gla, gla_norm_g)).astype(h.dtype) * jax.nn.silu(g_gate)

    lb = jnp.cumsum(jax.nn.softmax(hgrn_lb_logits.astype(f32), axis=0), axis=0)[layer]
    forget = lb + (1.0 - lb) * jax.nn.sigmoid(hf.astype(f32))
    o_hgrn = chunk_gated_linear_attention(
        split_heads(jax.nn.silu(hq.astype(f32)), HGRN_HEADS),
        split_heads(1.0 - forget, HGRN_HEADS),
        split_heads(hi.astype(f32), HGRN_HEADS),
        split_heads(jnp.log(forget), HGRN_HEADS), CHUNK)
    o_hgrn = merge_heads(rms_norm(o_hgrn, hgrn_norm_g)).astype(h.dtype) * jax.nn.silu(h_gate)

    return jnp.concatenate([o_gla, o_hgrn], axis=-1) @ w_out


def setup_inputs(seed: int = 0) -> dict:
    key = jax.random.key(seed)
    ks = jax.random.split(key, 14)
    nrm = jax.random.normal
    f32 = jnp.float32
    return {
        "x": nrm(ks[0], (BATCH, SEQ, D_MODEL), f32),
        "c": nrm(ks[1], (BATCH, D_MODEL), f32),
        "w_ada": nrm(ks[2], (DEPTH, D_MODEL, N_MOD * D_MODEL), f32) * (D_MODEL ** -0.5),
        "b_ada": nrm(ks[3], (DEPTH, N_MOD * D_MODEL), f32) * 0.02,
        "w_in": nrm(ks[4], (DEPTH, D_MODEL, IN_WIDTH), f32) * (D_MODEL ** -0.5),
        "gla_gate_w2": nrm(ks[5], (DEPTH, GLA_GATE_RANK, GLA_QK), f32) * (GLA_GATE_RANK ** -0.5),
        "gla_gate_b2": nrm(ks[6], (DEPTH, GLA_QK), f32) * 0.02,
        "gla_norm_g": 1.0 + 0.02 * nrm(ks[7], (DEPTH, GLA_DV), f32),
        "hgrn_lb_logits": nrm(ks[8], (DEPTH + 1, HGRN_K), f32) * 0.1,
        "hgrn_norm_g": 1.0 + 0.02 * nrm(ks[9], (DEPTH, HGRN_DV), f32),
        "w_out": nrm(ks[10], (DEPTH, MIX_WIDTH, D_MODEL), f32) * (MIX_WIDTH ** -0.5),
        "w_mlp1": nrm(ks[11], (DEPTH, D_MODEL, D_FF), f32) * (D_MODEL ** -0.5),
        "w_mlp2": nrm(ks[12], (DEPTH, D_FF, D_MODEL), f32) * (D_FF ** -0.5),
        "final_norm_g": 1.0 + 0.02 * nrm(ks[13], (D_MODEL,), f32),
    }


def reference(x, c, w_ada, b_ada, w_in, gla_gate_w2, gla_gate_b2, gla_norm_g,
              hgrn_lb_logits, hgrn_norm_g, w_out, w_mlp1, w_mlp2, final_norm_g):
    cond = jax.nn.silu(c)
    for layer in range(DEPTH):
        mod = cond @ w_ada[layer] + b_ada[layer]
        shift1, scale1, gate1, shift2, scale2, gate2 = jnp.split(mod, N_MOD, axis=-1)
        h = modulate(rms_norm(x), shift1, scale1)
        y = hybrid_mixer(h, layer, w_in[layer], gla_gate_w2[layer], gla_gate_b2[layer],
                         gla_norm_g[layer], hgrn_lb_logits, hgrn_norm_g[layer], w_out[layer])
        x = x + gate1[:, None, :] * y
        h = modulate(rms_norm(x), shift2, scale2)
        y = jnp.square(jax.nn.relu(h @ w_mlp1[layer])) @ w_mlp2[layer]
        x = x + gate2[:, None, :] * y
    return rms_norm(x, final_norm_g)
```

```python
import functools

import jax
import jax.numpy as jnp
from jax import lax
from jax.experimental import pallas as pl
from jax.experimental.pallas import tpu as pltpu

F32 = jnp.float32
BF16 = jnp.bfloat16

D_MODEL = 1024
GLA_HEADS = 4
GLA_DK = 64
GLA_DV = 128
GLA_QK = GLA_HEADS * GLA_DK
GLA_V = GLA_HEADS * GLA_DV
GLA_GATE_RANK = 16
GLA_GATE_NORMALIZER = 16.0
HGRN_HEADS = 4
HGRN_DK = 128
HGRN_DV = 128
HGRN_K = HGRN_HEADS * HGRN_DK
HGRN_V = HGRN_HEADS * HGRN_DV
CHUNK = 64
D_FF = 4 * D_MODEL
N_MOD = 6
EPS = 1e-6

LANES = 128
RANK_PAD = LANES

OFF_GQ = 0
OFF_GK = OFF_GQ + GLA_QK
OFF_GV = OFF_GK + GLA_QK
OFF_GG = OFF_GV + GLA_V
OFF_HQ = OFF_GG + GLA_V
OFF_HF = OFF_HQ + HGRN_K
OFF_HI = OFF_HF + HGRN_K
OFF_HG = OFF_HI + HGRN_V
OFF_LR = OFF_HG + HGRN_V
IN_WIDTH_P = OFF_LR + RANK_PAD

MIX_TILE = 256
MLP_TILE = 512
FF_BLOCK = 1024
VMEM_LIMIT = 56 * 1024 * 1024


def _rms(x):
    return x * lax.rsqrt(jnp.mean(x * x, axis=-1, keepdims=True) + EPS)


def _sigmoid(x):
    return 1.0 / (1.0 + jnp.exp(-x))


def _silu(x):
    return x * _sigmoid(x)


def _log_sigmoid(x):
    return jnp.minimum(x, 0.0) - jnp.log(1.0 + jnp.exp(-jnp.abs(x)))


def _dot(a, b):
    return jnp.dot(a, b, preferred_element_type=F32)


def _dot_nt(a, b):
    return lax.dot_general(a, b, (((1,), (1,)), ((), ())), preferred_element_type=F32)


def _mod_kernel(c_ref, w_ref, b_ref, o_ref):
    cond = _silu(c_ref[...])
    o_ref[...] = _dot(cond.astype(BF16), w_ref[...].astype(BF16)) + b_ref[...]


def _modulation(c, w_ada, b_ada):
    batch = c.shape[0]
    n_out = w_ada.shape[1]
    blk = D_MODEL
    return pl.pallas_call(
        _mod_kernel,
        grid=(n_out // blk,),
        in_specs=[
            pl.BlockSpec((batch, D_MODEL), lambda j: (0, 0)),
            pl.BlockSpec((D_MODEL, blk), lambda j: (0, j)),
            pl.BlockSpec((1, blk), lambda j: (0, j)),
        ],
        out_specs=pl.BlockSpec((batch, blk), lambda j: (0, j)),
        out_shape=jax.ShapeDtypeStruct((batch, n_out), F32),
        name="adaln_mod",
    )(c, w_ada, b_ada.reshape(1, n_out))


def _chunk_cumsum(x):
    row_in_chunk = lax.broadcasted_iota(jnp.int32, x.shape, 0) % CHUNK
    s = 1
    while s < CHUNK:
        x = x + jnp.where(row_in_chunk >= s, pltpu.roll(x, s, 0), 0.0)
        s *= 2
    return x


def _chunk_row_bcast(x, r):
    n_chunks = x.shape[0] // CHUNK
    parts = [jnp.broadcast_to(x[j * CHUNK + r:j * CHUNK + r + 1, :], (CHUNK, x.shape[1]))
             for j in range(n_chunks)]
    return jnp.concatenate(parts, axis=0)


def _gated_linear_attention(q, k, v, log_a, s_ref, n_heads, dk, dv, causal):
    tile = q.shape[0]
    n_chunks = tile // CHUNK
    width = n_heads * dk
    col_chunk = lax.broadcasted_iota(jnp.int32, (dk, tile), 1) // CHUNK
    head_of_lane = lax.broadcasted_iota(jnp.int32, (tile, width), 1) // dk

    cum = _chunk_cumsum(log_a)
    cum_mid = _chunk_row_bcast(cum, CHUNK // 2 - 1)
    cum_last = _chunk_row_bcast(cum, CHUNK - 1)
    q_rel = q * jnp.exp(cum - cum_mid)
    k_rel = k * jnp.exp(cum_mid - cum)
    k_end = k * jnp.exp(cum_last - cum)
    q_in = (q * jnp.exp(cum)).astype(BF16)

    k_end_t = k_end.T
    rows8 = lax.broadcasted_iota(jnp.int32, (8, width), 0)
    last8 = jnp.zeros((8, width), F32)
    for j in range(n_chunks):
        last8 = jnp.where(rows8 == j, cum[j * CHUNK + CHUNK - 1:j * CHUNK + CHUNK, :], last8)
    last_pad = jnp.concatenate([last8, jnp.zeros((LANES - 8, width), F32)], axis=0)
    decay_t = jnp.exp(last_pad.T)

    v_b = v.astype(BF16)
    k_rel_b = k_rel.astype(BF16)

    intra = []
    states = [[] for _ in range(n_chunks)]
    for h in range(n_heads):
        v_h = v_b[:, h * dv:(h + 1) * dv]
        if dk % LANES == 0:
            scores = _dot_nt(q_rel[:, h * dk:(h + 1) * dk].astype(BF16),
                             k_rel_b[:, h * dk:(h + 1) * dk])
        else:
            q_h = jnp.where(head_of_lane == h, q_rel, 0.0).astype(BF16)
            scores = _dot_nt(q_h, k_rel_b)
        p = jnp.where(causal, scores, 0.0).astype(BF16)
        intra.append(_dot(p, v_h))
        k_t_h = k_end_t[h * dk:(h + 1) * dk, :]
        k_blk = jnp.concatenate(
            [jnp.where(col_chunk == j, k_t_h, 0.0) for j in range(n_chunks)], axis=0)
        upd = _dot(k_blk.astype(BF16), v_h)
        s = s_ref[h]
        for j in range(n_chunks):
            states[j].append(s)
            s = decay_t[h * dk:(h + 1) * dk, j:j + 1] * s + upd[j * dk:(j + 1) * dk, :]
        s_ref[h] = s

    inter = []
    for j in range(n_chunks):
        rows = []
        for h in range(n_heads):
            blocks = [states[j][h].astype(BF16) if g == h else jnp.zeros((dk, dv), BF16)
                      for g in range(n_heads)]
            rows.append(jnp.concatenate(blocks, axis=1))
        w_state = jnp.concatenate(rows, axis=0)
        inter.append(_dot(q_in[j * CHUNK:(j + 1) * CHUNK, :], w_state))
    return jnp.concatenate(intra, axis=1) + jnp.concatenate(inter, axis=0)


def _head_norm_gate(o, gain, gate, n_heads, dv):
    outs = []
    for h in range(n_heads):
        o_h = o[:, h * dv:(h + 1) * dv]
        outs.append(_rms(o_h) * gain)
    return jnp.concatenate(outs, axis=1) * _silu(gate)


def _mixer_kernel(x_ref, mod_ref, w_in_ref, w2_ref, b2_ref, gla_g_ref, lb_ref, hgrn_g_ref,
                  w_out_ref, o_ref, s_gla_ref, s_hgrn_ref):
    @pl.when(pl.program_id(1) == 0)
    def _():
        s_gla_ref[...] = jnp.zeros_like(s_gla_ref)
        s_hgrn_ref[...] = jnp.zeros_like(s_hgrn_ref)

    tile = x_ref.shape[1]
    x = x_ref[0]
    shift = mod_ref[0, 0:1, :]
    scale = mod_ref[0, 1:2, :]
    gate = mod_ref[0, 2:3, :]
    h_b = (_rms(x) * (1.0 + scale) + shift).astype(BF16)

    def proj(off, width):
        return _dot(h_b, w_in_ref[:, off:off + width])

    row = lax.broadcasted_iota(jnp.int32, (tile, tile), 0)
    col = lax.broadcasted_iota(jnp.int32, (tile, tile), 1)
    causal = (row // CHUNK == col // CHUNK) & (col <= row)

    gate_logits = _dot(proj(OFF_LR, RANK_PAD).astype(BF16), w2_ref[...]) + b2_ref[...]
    log_a = _log_sigmoid(gate_logits) * (1.0 / GLA_GATE_NORMALIZER)
    o_gla = _gated_linear_attention(
        proj(OFF_GQ, GLA_QK) * (GLA_DK ** -0.5), proj(OFF_GK, GLA_QK), proj(OFF_GV, GLA_V),
        log_a, s_gla_ref, GLA_HEADS, GLA_DK, GLA_DV, causal)
    o_gla = _head_norm_gate(o_gla, gla_g_ref[...], proj(OFF_GG, GLA_V), GLA_HEADS, GLA_DV)

    lb_e = jnp.exp(lb_ref[...] - jnp.max(lb_ref[...], axis=0, keepdims=True))
    lb = lb_e[0:1, :] / jnp.sum(lb_e, axis=0, keepdims=True)
    forget = lb + (1.0 - lb) * _sigmoid(proj(OFF_HF, HGRN_K))
    o_hgrn = _gated_linear_attention(
        _silu(proj(OFF_HQ, HGRN_K)), 1.0 - forget, proj(OFF_HI, HGRN_V), jnp.log(forget),
        s_hgrn_ref, HGRN_HEADS, HGRN_DK, HGRN_DV, causal)
    o_hgrn = _head_norm_gate(o_hgrn, hgrn_g_ref[...], proj(OFF_HG, HGRN_V), HGRN_HEADS, HGRN_DV)

    mixed = jnp.concatenate([o_gla, o_hgrn], axis=1).astype(BF16)
    o_ref[0] = x + gate * _dot(mixed, w_out_ref[...])


def _const_spec(shape):
    zeros = (0,) * len(shape)
    return pl.BlockSpec(shape, lambda *_: zeros, pipeline_mode=pl.Buffered(1))


def _mixer(x, mod, w_in_p, w2_p, b2, gla_g, lb_logits, hgrn_g, w_out_b):
    batch, seq, _ = x.shape
    tile = MIX_TILE
    return pl.pallas_call(
        _mixer_kernel,
        grid=(batch, seq // tile),
        in_specs=[
            pl.BlockSpec((1, tile, D_MODEL), lambda b, t: (b, t, 0)),
            pl.BlockSpec((1, N_MOD, D_MODEL), lambda b, t: (b, 0, 0)),
            _const_spec(w_in_p.shape),
            _const_spec(w2_p.shape),
            _const_spec(b2.shape),
            _const_spec(gla_g.shape),
            _const_spec(lb_logits.shape),
            _const_spec(hgrn_g.shape),
            _const_spec(w_out_b.shape),
        ],
        out_specs=pl.BlockSpec((1, tile, D_MODEL), lambda b, t: (b, t, 0)),
        out_shape=jax.ShapeDtypeStruct(x.shape, F32),
        scratch_shapes=[
            pltpu.VMEM((GLA_HEADS, GLA_DK, GLA_DV), F32),
            pltpu.VMEM((HGRN_HEADS, HGRN_DK, HGRN_DV), F32),
        ],
        compiler_params=pltpu.CompilerParams(
            dimension_semantics=("arbitrary", "arbitrary"), vmem_limit_bytes=VMEM_LIMIT),
        name="token_mixer",
    )(x, mod, w_in_p, w2_p, b2, gla_g, lb_logits, hgrn_g, w_out_b)


def _mlp_kernel(x_ref, mod_ref, w1_ref, w2_ref, g_ref, o_ref):
    x = x_ref[0]
    shift = mod_ref[0, 3:4, :]
    scale = mod_ref[0, 4:5, :]
    gate = mod_ref[0, 5:6, :]
    h_b = (_rms(x) * (1.0 + scale) + shift).astype(BF16)
    acc = jnp.zeros(x.shape, F32)
    for j in range(D_FF // FF_BLOCK):
        a = jnp.maximum(_dot(h_b, w1_ref[:, j * FF_BLOCK:(j + 1) * FF_BLOCK]), 0.0)
        acc = acc + _dot((a * a).astype(BF16), w2_ref[j * FF_BLOCK:(j + 1) * FF_BLOCK, :])
    o_ref[0] = _rms(x + gate * acc) * g_ref[...]


def _mlp(x, mod, w1_b, w2_b, final_g):
    batch, seq, _ = x.shape
    tile = MLP_TILE
    return pl.pallas_call(
        _mlp_kernel,
        grid=(batch, seq // tile),
        in_specs=[
            pl.BlockSpec((1, tile, D_MODEL), lambda b, t: (b, t, 0)),
            pl.BlockSpec((1, N_MOD, D_MODEL), lambda b, t: (b, 0, 0)),
            _const_spec(w1_b.shape),
            _const_spec(w2_b.shape),
            _const_spec(final_g.shape),
        ],
        out_specs=pl.BlockSpec((1, tile, D_MODEL), lambda b, t: (b, t, 0)),
        out_shape=jax.ShapeDtypeStruct(x.shape, F32),
        compiler_params=pltpu.CompilerParams(
            dimension_semantics=("arbitrary", "arbitrary"), vmem_limit_bytes=VMEM_LIMIT),
        name="channel_mixer",
    )(x, mod, w1_b, w2_b, final_g)


def _permute_w_in(w_in):
    sizes = (GLA_QK, GLA_QK, GLA_V, GLA_GATE_RANK, GLA_V, HGRN_K, HGRN_K, HGRN_V, HGRN_V)
    offs = [0]
    for s in sizes:
        offs.append(offs[-1] + s)
    gq, gk, gv, g_lr, g_gate, hq, hf, hi, h_gate = [w_in[:, offs[i]:offs[i + 1]] for i in range(9)]
    pad = jnp.zeros((w_in.shape[0], RANK_PAD - GLA_GATE_RANK), w_in.dtype)
    return jnp.concatenate([gq, gk, gv, g_gate, hq, hf, hi, h_gate, g_lr, pad], axis=1)


def kernel(x, c, w_ada, b_ada, w_in, gla_gate_w2, gla_gate_b2, gla_norm_g, hgrn_lb_logits,
           hgrn_norm_g, w_out, w_mlp1, w_mlp2, final_norm_g):
    assert w_ada.shape[0] == 1, "single-layer trunk"
    batch = x.shape[0]
    mod = _modulation(c, w_ada[0], b_ada[0]).reshape(batch, N_MOD, D_MODEL)

    w_in_p = _permute_w_in(w_in[0]).astype(BF16)
    w2_p = jnp.concatenate(
        [gla_gate_w2[0], jnp.zeros((RANK_PAD - GLA_GATE_RANK, GLA_QK), F32)], axis=0).astype(BF16)
    x1 = _mixer(x, mod, w_in_p, w2_p, gla_gate_b2[0].reshape(1, GLA_QK),
                gla_norm_g[0].reshape(1, GLA_DV), hgrn_lb_logits,
                hgrn_norm_g[0].reshape(1, HGRN_DV), w_out[0].astype(BF16))
    return _mlp(x1, mod, w_mlp1[0].astype(BF16), w_mlp2[0].astype(BF16),
                final_norm_g.reshape(1, D_MODEL))
```

```python
import functools

import jax
import jax.numpy as jnp
from jax import lax
from jax.experimental import pallas as pl
from jax.experimental.pallas import tpu as pltpu

F32 = jnp.float32
BF16 = jnp.bfloat16

D_MODEL = 1024
GLA_HEADS = 4
GLA_DK = 64
GLA_DV = 128
GLA_QK = GLA_HEADS * GLA_DK
GLA_V = GLA_HEADS * GLA_DV
GLA_GATE_RANK = 16
GLA_GATE_NORMALIZER = 16.0
HGRN_HEADS = 4
HGRN_DK = 128
HGRN_DV = 128
HGRN_K = HGRN_HEADS * HGRN_DK
HGRN_V = HGRN_HEADS * HGRN_DV
CHUNK = 64
D_FF = 4 * D_MODEL
N_MOD = 6
EPS = 1e-6

LANES = 128
RANK_PAD = LANES

OFF_GQ = 0
OFF_GK = OFF_GQ + GLA_QK
OFF_GV = OFF_GK + GLA_QK
OFF_GG = OFF_GV + GLA_V
OFF_HQ = OFF_GG + GLA_V
OFF_HF = OFF_HQ + HGRN_K
OFF_HI = OFF_HF + HGRN_K
OFF_HG = OFF_HI + HGRN_V
OFF_LR = OFF_HG + HGRN_V
IN_WIDTH_P = OFF_LR + RANK_PAD

TILE = 256
FF_BLOCK = 512
VMEM_LIMIT = 56 * 1024 * 1024


def _rms(x):
    return x * lax.rsqrt(jnp.mean(x * x, axis=-1, keepdims=True) + EPS)


def _sigmoid(x):
    return 1.0 / (1.0 + jnp.exp(-x))


def _silu(x):
    return x * _sigmoid(x)


def _log_sigmoid(x):
    return jnp.minimum(x, 0.0) - jnp.log(1.0 + jnp.exp(-jnp.abs(x)))


def _mod_kernel(c_ref, w_ref, b_ref, o_ref):
    cond = _silu(c_ref[...])
    o_ref[...] = jnp.dot(cond.astype(BF16), w_ref[...].astype(BF16),
                         preferred_element_type=F32) + b_ref[...]


def _modulation(c, w_ada, b_ada):
    batch = c.shape[0]
    n_out = w_ada.shape[1]
    blk = D_MODEL
    return pl.pallas_call(
        _mod_kernel,
        grid=(n_out // blk,),
        in_specs=[
            pl.BlockSpec((batch, D_MODEL), lambda j: (0, 0)),
            pl.BlockSpec((D_MODEL, blk), lambda j: (0, j)),
            pl.BlockSpec((1, blk), lambda j: (0, j)),
        ],
        out_specs=pl.BlockSpec((batch, blk), lambda j: (0, j)),
        out_shape=jax.ShapeDtypeStruct((batch, n_out), F32),
        name="adaln_mod",
    )(c, w_ada, b_ada.reshape(1, n_out))


def _chunk_cumsum(x):
    row_in_chunk = lax.broadcasted_iota(jnp.int32, x.shape, 0) % CHUNK
    s = 1
    while s < CHUNK:
        x = x + jnp.where(row_in_chunk >= s, pltpu.roll(x, s, 0), 0.0)
        s *= 2
    return x


def _chunk_row_bcast(x, r):
    n_chunks = x.shape[0] // CHUNK
    parts = [jnp.broadcast_to(x[j * CHUNK + r:j * CHUNK + r + 1, :], (CHUNK, x.shape[1]))
             for j in range(n_chunks)]
    return jnp.concatenate(parts, axis=0)


def _gated_linear_attention(q, k, v, log_a, s_ref, n_heads, dk, dv, causal, fill):
    tile = q.shape[0]
    n_chunks = tile // CHUNK
    width = n_heads * dk
    col_chunk = lax.broadcasted_iota(jnp.int32, (dk, tile), 1) // CHUNK
    head_of_lane = lax.broadcasted_iota(jnp.int32, (tile, width), 1) // dk

    cum = _chunk_cumsum(log_a)
    cum_mid = _chunk_row_bcast(cum, CHUNK // 2 - 1)
    cum_last = _chunk_row_bcast(cum, CHUNK - 1)
    q_rel = q * jnp.exp(cum - cum_mid)
    k_rel = k * jnp.exp(cum_mid - cum)
    k_end = k * jnp.exp(cum_last - cum)
    q_in = (q * jnp.exp(cum)).astype(BF16)

    k_end_t = k_end.T
    rows8 = lax.broadcasted_iota(jnp.int32, (8, width), 0)
    last8 = jnp.zeros((8, width), F32)
    for j in range(n_chunks):
        last8 = jnp.where(rows8 == j, cum[j * CHUNK + CHUNK - 1:j * CHUNK + CHUNK, :], last8)
    last_pad = jnp.concatenate([last8, jnp.zeros((LANES - 8, width), F32)], axis=0)
    decay_t = jnp.exp(last_pad.T)

    v_b = v.astype(BF16)
    k_rel_b = k_rel.astype(BF16)
    fill(2)

    intra = []
    states = [[] for _ in range(n_chunks)]
    for h in range(n_heads):
        v_h = v_b[:, h * dv:(h + 1) * dv]
        if dk % LANES == 0:
            scores = lax.dot_general(
                q_rel[:, h * dk:(h + 1) * dk].astype(BF16), k_rel_b[:, h * dk:(h + 1) * dk],
                (((1,), (1,)), ((), ())), preferred_element_type=F32)
        else:
            q_h = jnp.where(head_of_lane == h, q_rel, 0.0).astype(BF16)
            scores = lax.dot_general(q_h, k_rel_b, (((1,), (1,)), ((), ())),
                                     preferred_element_type=F32)
        fill(1)
        p = jnp.where(causal, scores, 0.0).astype(BF16)
        intra.append(jnp.dot(p, v_h, preferred_element_type=F32))
        k_t_h = k_end_t[h * dk:(h + 1) * dk, :]
        k_blk = jnp.concatenate(
            [jnp.where(col_chunk == j, k_t_h, 0.0) for j in range(n_chunks)], axis=0)
        upd = jnp.dot(k_blk.astype(BF16), v_h, preferred_element_type=F32)
        s = s_ref[h]
        for j in range(n_chunks):
            states[j].append(s)
            s = decay_t[h * dk:(h + 1) * dk, j:j + 1] * s + upd[j * dk:(j + 1) * dk, :]
        s_ref[h] = s

    inter = []
    for j in range(n_chunks):
        rows = []
        for h in range(n_heads):
            blocks = [states[j][h].astype(BF16) if g == h else jnp.zeros((dk, dv), BF16)
                      for g in range(n_heads)]
            rows.append(jnp.concatenate(blocks, axis=1))
        w_state = jnp.concatenate(rows, axis=0)
        inter.append(jnp.dot(q_in[j * CHUNK:(j + 1) * CHUNK, :], w_state,
                             preferred_element_type=F32))
    return jnp.concatenate(intra, axis=1) + jnp.concatenate(inter, axis=0)


def _head_norm_gate(o, gain, gate, n_heads, dv):
    outs = []
    for h in range(n_heads):
        o_h = o[:, h * dv:(h + 1) * dv]
        outs.append(_rms(o_h) * gain)
    return jnp.concatenate(outs, axis=1) * _silu(gate)


def _token_mixer(x, mod_ref, w_in_ref, w2_ref, b2_ref, gla_g_ref, lb_ref, hgrn_g_ref, w_out_ref,
                 s_gla_ref, s_hgrn_ref, fill):
    tile = x.shape[0]
    shift = mod_ref[0, 0:1, :]
    scale = mod_ref[0, 1:2, :]
    gate = mod_ref[0, 2:3, :]
    h_b = (_rms(x) * (1.0 + scale) + shift).astype(BF16)
    fill(1)

    def proj(off, width):
        return jnp.dot(h_b, w_in_ref[:, off:off + width], preferred_element_type=F32)

    row = lax.broadcasted_iota(jnp.int32, (tile, tile), 0)
    col = lax.broadcasted_iota(jnp.int32, (tile, tile), 1)
    causal = (row // CHUNK == col // CHUNK) & (col <= row)

    gate_logits = jnp.dot(proj(OFF_LR, RANK_PAD).astype(BF16), w2_ref[...],
                          preferred_element_type=F32) + b2_ref[...]
    log_a = _log_sigmoid(gate_logits) * (1.0 / GLA_GATE_NORMALIZER)
    o_gla = _gated_linear_attention(
        proj(OFF_GQ, GLA_QK) * (GLA_DK ** -0.5), proj(OFF_GK, GLA_QK), proj(OFF_GV, GLA_V),
        log_a, s_gla_ref, GLA_HEADS, GLA_DK, GLA_DV, causal, fill)
    o_gla = _head_norm_gate(o_gla, gla_g_ref[...], proj(OFF_GG, GLA_V), GLA_HEADS, GLA_DV)
    fill(1)

    lb_e = jnp.exp(lb_ref[...] - jnp.max(lb_ref[...], axis=0, keepdims=True))
    lb = lb_e[0:1, :] / jnp.sum(lb_e, axis=0, keepdims=True)
    forget = lb + (1.0 - lb) * _sigmoid(proj(OFF_HF, HGRN_K))
    o_hgrn = _gated_linear_attention(
        _silu(proj(OFF_HQ, HGRN_K)), 1.0 - forget, proj(OFF_HI, HGRN_V), jnp.log(forget),
        s_hgrn_ref, HGRN_HEADS, HGRN_DK, HGRN_DV, causal, fill)
    o_hgrn = _head_norm_gate(o_hgrn, hgrn_g_ref[...], proj(OFF_HG, HGRN_V), HGRN_HEADS, HGRN_DV)
    fill(1)

    mixed = jnp.concatenate([o_gla, o_hgrn], axis=1).astype(BF16)
    return x + gate * jnp.dot(mixed, w_out_ref[...], preferred_element_type=F32)


class _ChannelMixer:
    def __init__(self, x, mod_ref, w1_ref, w2_ref, g_ref):
        self.x, self.w1_ref, self.w2_ref, self.g_ref = x, w1_ref, w2_ref, g_ref
        shift = mod_ref[0, 3:4, :]
        scale = mod_ref[0, 4:5, :]
        self.gate = mod_ref[0, 5:6, :]
        self.h_b = (_rms(x) * (1.0 + scale) + shift).astype(BF16)
        self.acc = None
        self.hidden = None
        self.block = 0

    def step(self):
        lo = self.block * FF_BLOCK
        if lo >= D_FF:
            return
        if self.hidden is None:
            a = jnp.maximum(jnp.dot(self.h_b, self.w1_ref[:, lo:lo + FF_BLOCK],
                                    preferred_element_type=F32), 0.0)
            self.hidden = (a * a).astype(BF16)
        else:
            y = jnp.dot(self.hidden, self.w2_ref[lo:lo + FF_BLOCK, :], preferred_element_type=F32)
            self.acc = y if self.acc is None else self.acc + y
            self.hidden = None
            self.block += 1

    def fill(self, n):
        for _ in range(n):
            self.step()

    def finish(self):
        while self.block * FF_BLOCK < D_FF:
            self.step()
        return _rms(self.x + self.gate * self.acc) * self.g_ref[...]


def _layer_kernel(tiles_per_seq, x_ref, mod_a_ref, mod_b_ref, w_in_ref, w2_ref, b2_ref, gla_g_ref,
                  lb_ref, hgrn_g_ref, w_out_ref, w_mlp1_ref, w_mlp2_ref, final_g_ref, o_ref,
                  s_gla_ref, s_hgrn_ref, x1_ref):
    i = pl.program_id(0)

    @pl.when(i == 0)
    def _():
        x1_ref[...] = jnp.zeros_like(x1_ref)

    @pl.when(i % tiles_per_seq == 0)
    def _():
        s_gla_ref[...] = jnp.zeros_like(s_gla_ref)
        s_hgrn_ref[...] = jnp.zeros_like(s_hgrn_ref)

    mlp = _ChannelMixer(x1_ref[(i + 1) % 2], mod_b_ref, w_mlp1_ref, w_mlp2_ref, final_g_ref)
    x1_ref[i % 2] = _token_mixer(x_ref[0], mod_a_ref, w_in_ref, w2_ref, b2_ref, gla_g_ref, lb_ref,
                                 hgrn_g_ref, w_out_ref, s_gla_ref, s_hgrn_ref, mlp.fill)
    o_ref[0] = mlp.finish()


def _const_spec(shape):
    zeros = (0,) * len(shape)
    return pl.BlockSpec(shape, lambda *_: zeros, pipeline_mode=pl.Buffered(1))


def _layer(x, mod, w_in_p, w2_p, b2, gla_g, lb_logits, hgrn_g, w_out_b, w_mlp1_b, w_mlp2_b,
           final_g):
    batch, seq, _ = x.shape
    tps = seq // TILE
    n_tiles = batch * tps

    def tile_in(i):
        j = jnp.minimum(i, n_tiles - 1)
        return (j // tps, j % tps, 0)

    def tile_out(i):
        j = jnp.maximum(i - 1, 0)
        return (j // tps, j % tps, 0)

    return pl.pallas_call(
        functools.partial(_layer_kernel, tps),
        grid=(n_tiles + 1,),
        in_specs=[
            pl.BlockSpec((1, TILE, D_MODEL), tile_in),
            pl.BlockSpec((1, N_MOD, D_MODEL), lambda i: (tile_in(i)[0], 0, 0)),
            pl.BlockSpec((1, N_MOD, D_MODEL), lambda i: (tile_out(i)[0], 0, 0)),
            _const_spec(w_in_p.shape),
            _const_spec(w2_p.shape),
            _const_spec(b2.shape),
            _const_spec(gla_g.shape),
            _const_spec(lb_logits.shape),
            _const_spec(hgrn_g.shape),
            _const_spec(w_out_b.shape),
            _const_spec(w_mlp1_b.shape),
            _const_spec(w_mlp2_b.shape),
            _const_spec(final_g.shape),
        ],
        out_specs=pl.BlockSpec((1, TILE, D_MODEL), tile_out),
        out_shape=jax.ShapeDtypeStruct(x.shape, F32),
        scratch_shapes=[
            pltpu.VMEM((GLA_HEADS, GLA_DK, GLA_DV), F32),
            pltpu.VMEM((HGRN_HEADS, HGRN_DK, HGRN_DV), F32),
            pltpu.VMEM((2, TILE, D_MODEL), F32),
        ],
        compiler_params=pltpu.CompilerParams(
            dimension_semantics=("arbitrary",), vmem_limit_bytes=VMEM_LIMIT),
        name="hybrid_layer",
    )(x, mod, mod, w_in_p, w2_p, b2, gla_g, lb_logits, hgrn_g, w_out_b, w_mlp1_b, w_mlp2_b,
      final_g)


def _permute_w_in(w_in):
    sizes = (GLA_QK, GLA_QK, GLA_V, GLA_GATE_RANK, GLA_V, HGRN_K, HGRN_K, HGRN_V, HGRN_V)
    offs = [0]
    for s in sizes:
        offs.append(offs[-1] + s)
    gq, gk, gv, g_lr, g_gate, hq, hf, hi, h_gate = [w_in[:, offs[i]:offs[i + 1]] for i in range(9)]
    pad = jnp.zeros((w_in.shape[0], RANK_PAD - GLA_GATE_RANK), w_in.dtype)
    return jnp.concatenate([gq, gk, gv, g_gate, hq, hf, hi, h_gate, g_lr, pad], axis=1)


def kernel(x, c, w_ada, b_ada, w_in, gla_gate_w2, gla_gate_b2, gla_norm_g, hgrn_lb_logits,
           hgrn_norm_g, w_out, w_mlp1, w_mlp2, final_norm_g):
    assert w_ada.shape[0] == 1, "single-layer trunk"
    batch = x.shape[0]
    mod = _modulation(c, w_ada[0], b_ada[0]).reshape(batch, N_MOD, D_MODEL)

    w_in_p = _permute_w_in(w_in[0]).astype(BF16)
    w2_p = jnp.concatenate(
        [gla_gate_w2[0], jnp.zeros((RANK_PAD - GLA_GATE_RANK, GLA_QK), F32)], axis=0).astype(BF16)
    return _layer(x, mod, w_in_p, w2_p, gla_gate_b2[0].reshape(1, GLA_QK),
                  gla_norm_g[0].reshape(1, GLA_DV), hgrn_lb_logits,
                  hgrn_norm_g[0].reshape(1, HGRN_DV), w_out[0].astype(BF16),
                  w_mlp1[0].astype(BF16), w_mlp2[0].astype(BF16),
                  final_norm_g.reshape(1, D_MODEL))
```

```python
import functools

import jax
import jax.numpy as jnp
from jax import lax
from jax.experimental import pallas as pl
from jax.experimental.pallas import tpu as pltpu

F32 = jnp.float32
BF16 = jnp.bfloat16

D_MODEL = 1024
GLA_HEADS = 4
GLA_DK = 64
GLA_DV = 128
GLA_QK = GLA_HEADS * GLA_DK
GLA_V = GLA_HEADS * GLA_DV
GLA_GATE_RANK = 16
GLA_GATE_NORMALIZER = 16.0
HGRN_HEADS = 4
HGRN_DK = 128
HGRN_DV = 128
HGRN_K = HGRN_HEADS * HGRN_DK
HGRN_V = HGRN_HEADS * HGRN_DV
CHUNK = 64
D_FF = 4 * D_MODEL
N_MOD = 6
EPS = 1e-6

LANES = 128
RANK_PAD = LANES

OFF_GQ = 0
OFF_GK = OFF_GQ + GLA_QK
OFF_GV = OFF_GK + GLA_QK
OFF_GG = OFF_GV + GLA_V
OFF_HQ = OFF_GG + GLA_V
OFF_HF = OFF_HQ + HGRN_K
OFF_HI = OFF_HF + HGRN_K
OFF_HG = OFF_HI + HGRN_V
OFF_LR = OFF_HG + HGRN_V
IN_WIDTH_P = OFF_LR + RANK_PAD

TILE = 512
MIX_ROWS = 256
FF_BLOCK = 256
VMEM_LIMIT = 56 * 1024 * 1024


def _rms(x):
    return x * lax.rsqrt(jnp.mean(x * x, axis=-1, keepdims=True) + EPS)


def _sigmoid(x):
    return 1.0 / (1.0 + jnp.exp(-x))


def _silu(x):
    return x * _sigmoid(x)


def _log_sigmoid(x):
    return jnp.minimum(x, 0.0) - jnp.log(1.0 + jnp.exp(-jnp.abs(x)))


def _mod_kernel(c_ref, w_ref, b_ref, o_ref):
    cond = _silu(c_ref[...])
    o_ref[...] = jnp.dot(cond.astype(BF16), w_ref[...].astype(BF16),
                         preferred_element_type=F32) + b_ref[...]


def _modulation(c, w_ada, b_ada):
    batch = c.shape[0]
    n_out = w_ada.shape[1]
    blk = D_MODEL
    return pl.pallas_call(
        _mod_kernel,
        grid=(n_out // blk,),
        in_specs=[
            pl.BlockSpec((batch, D_MODEL), lambda j: (0, 0)),
            pl.BlockSpec((D_MODEL, blk), lambda j: (0, j)),
            pl.BlockSpec((1, blk), lambda j: (0, j)),
        ],
        out_specs=pl.BlockSpec((batch, blk), lambda j: (0, j)),
        out_shape=jax.ShapeDtypeStruct((batch, n_out), F32),
        name="adaln_mod",
    )(c, w_ada, b_ada.reshape(1, n_out))


def _chunk_cumsum(x):
    row_in_chunk = lax.broadcasted_iota(jnp.int32, x.shape, 0) % CHUNK
    s = 1
    while s < CHUNK:
        x = x + jnp.where(row_in_chunk >= s, pltpu.roll(x, s, 0), 0.0)
        s *= 2
    return x


def _chunk_row_bcast(x, r):
    n_chunks = x.shape[0] // CHUNK
    parts = [jnp.broadcast_to(x[j * CHUNK + r:j * CHUNK + r + 1, :], (CHUNK, x.shape[1]))
             for j in range(n_chunks)]
    return jnp.concatenate(parts, axis=0)


def _gated_linear_attention(q, k, v, log_a, s_ref, n_heads, dk, dv, causal, fill):
    tile = q.shape[0]
    n_chunks = tile // CHUNK
    width = n_heads * dk
    col_chunk = lax.broadcasted_iota(jnp.int32, (dk, tile), 1) // CHUNK
    head_of_lane = lax.broadcasted_iota(jnp.int32, (tile, width), 1) // dk

    fill(2 + dk // LANES)
    cum = _chunk_cumsum(log_a)
    cum_mid = _chunk_row_bcast(cum, CHUNK // 2 - 1)
    cum_last = _chunk_row_bcast(cum, CHUNK - 1)
    q_rel = q * jnp.exp(cum - cum_mid)
    k_rel = k * jnp.exp(cum_mid - cum)
    k_end = k * jnp.exp(cum_last - cum)
    q_in = (q * jnp.exp(cum)).astype(BF16)

    k_end_t = k_end.T
    rows8 = lax.broadcasted_iota(jnp.int32, (8, width), 0)
    last8 = jnp.zeros((8, width), F32)
    for j in range(n_chunks):
        last8 = jnp.where(rows8 == j, cum[j * CHUNK + CHUNK - 1:j * CHUNK + CHUNK, :], last8)
    last_pad = jnp.concatenate([last8, jnp.zeros((LANES - 8, width), F32)], axis=0)
    decay_t = jnp.exp(last_pad.T)

    v_b = v.astype(BF16)
    k_rel_b = k_rel.astype(BF16)

    def head_scores(h):
        if dk % LANES == 0:
            return lax.dot_general(
                q_rel[:, h * dk:(h + 1) * dk].astype(BF16), k_rel_b[:, h * dk:(h + 1) * dk],
                (((1,), (1,)), ((), ())), preferred_element_type=F32)
        q_h = jnp.where(head_of_lane == h, q_rel, 0.0).astype(BF16)
        return lax.dot_general(q_h, k_rel_b, (((1,), (1,)), ((), ())),
                               preferred_element_type=F32)

    intra = []
    states = [[] for _ in range(n_chunks)]
    scores_next = head_scores(0)
    for h in range(n_heads):
        v_h = v_b[:, h * dv:(h + 1) * dv]
        scores = scores_next
        if h + 1 < n_heads:
            scores_next = head_scores(h + 1)
        fill(1)
        p = jnp.where(causal, scores, 0.0).astype(BF16)
        intra.append(jnp.dot(p, v_h, preferred_element_type=F32))
        k_t_h = k_end_t[h * dk:(h + 1) * dk, :]
        k_blk = jnp.concatenate(
            [jnp.where(col_chunk == j, k_t_h, 0.0) for j in range(n_chunks)], axis=0)
        upd = jnp.dot(k_blk.astype(BF16), v_h, preferred_element_type=F32)
        s = s_ref[h]
        for j in range(n_chunks):
            states[j].append(s)
            s = decay_t[h * dk:(h + 1) * dk, j:j + 1] * s + upd[j * dk:(j + 1) * dk, :]
        s_ref[h] = s

    inter = []
    for j in range(n_chunks):
        rows = []
        for h in range(n_heads):
            blocks = [states[j][h].astype(BF16) if g == h else jnp.zeros((dk, dv), BF16)
                      for g in range(n_heads)]
            rows.append(jnp.concatenate(blocks, axis=1))
        w_state = jnp.concatenate(rows, axis=0)
        inter.append(jnp.dot(q_in[j * CHUNK:(j + 1) * CHUNK, :], w_state,
                             preferred_element_type=F32))
    return jnp.concatenate(intra, axis=1) + jnp.concatenate(inter, axis=0)


def _head_norm_gate(o, gain, gate, n_heads, dv):
    outs = []
    for h in range(n_heads):
        o_h = o[:, h * dv:(h + 1) * dv]
        outs.append(_rms(o_h) * gain)
    return jnp.concatenate(outs, axis=1) * _silu(gate)


def _token_mixer(x, mod_ref, w_in_ref, w2_ref, b2_ref, gla_g_ref, lb_ref, hgrn_g_ref, w_out_ref,
                 s_gla_ref, s_hgrn_ref, fill, finish):
    tile = x.shape[0]
    shift = mod_ref[0, 0:1, :]
    scale = mod_ref[0, 1:2, :]
    gate = mod_ref[0, 2:3, :]
    fill(1)
    h_b = (_rms(x) * (1.0 + scale) + shift).astype(BF16)

    def proj(off, width):
        return jnp.dot(h_b, w_in_ref[:, off:off + width], preferred_element_type=F32)

    row = lax.broadcasted_iota(jnp.int32, (tile, tile), 0)
    col = lax.broadcasted_iota(jnp.int32, (tile, tile), 1)
    causal = (row // CHUNK == col // CHUNK) & (col <= row)

    gate_logits = jnp.dot(proj(OFF_LR, RANK_PAD).astype(BF16), w2_ref[...],
                          preferred_element_type=F32) + b2_ref[...]
    log_a = _log_sigmoid(gate_logits) * (1.0 / GLA_GATE_NORMALIZER)
    o_gla = _gated_linear_attention(
        proj(OFF_GQ, GLA_QK) * (GLA_DK ** -0.5), proj(OFF_GK, GLA_QK), proj(OFF_GV, GLA_V),
        log_a, s_gla_ref, GLA_HEADS, GLA_DK, GLA_DV, causal, fill)
    o_gla = _head_norm_gate(o_gla, gla_g_ref[...], proj(OFF_GG, GLA_V), GLA_HEADS, GLA_DV)
    fill(1)

    lb_e = jnp.exp(lb_ref[...] - jnp.max(lb_ref[...], axis=0, keepdims=True))
    lb = lb_e[0:1, :] / jnp.sum(lb_e, axis=0, keepdims=True)
    forget = lb + (1.0 - lb) * _sigmoid(proj(OFF_HF, HGRN_K))
    o_hgrn = _gated_linear_attention(
        _silu(proj(OFF_HQ, HGRN_K)), 1.0 - forget, proj(OFF_HI, HGRN_V), jnp.log(forget),
        s_hgrn_ref, HGRN_HEADS, HGRN_DK, HGRN_DV, causal, fill)
    o_hgrn = _head_norm_gate(o_hgrn, hgrn_g_ref[...], proj(OFF_HG, HGRN_V), HGRN_HEADS, HGRN_DV)
    finish()

    mixed = jnp.concatenate([o_gla, o_hgrn], axis=1).astype(BF16)
    return x + gate * jnp.dot(mixed, w_out_ref[...], preferred_element_type=F32)


class _ChannelMixer:
    def __init__(self, x, mod_ref, w1_ref, w2_ref, g_ref):
        self.x, self.w1_ref, self.w2_ref, self.g_ref = x, w1_ref, w2_ref, g_ref
        shift = mod_ref[0, 3:4, :]
        scale = mod_ref[0, 4:5, :]
        self.gate = mod_ref[0, 5:6, :]
        self.h_b = (_rms(x) * (1.0 + scale) + shift).astype(BF16)
        self.acc = None
        self.hidden = []
        self.up = 0
        self.down = 0

    def _up(self):
        lo = self.up * FF_BLOCK
        a = jnp.maximum(jnp.dot(self.h_b, self.w1_ref[:, lo:lo + FF_BLOCK],
                                preferred_element_type=F32), 0.0)
        self.hidden.append((a * a).astype(BF16))
        self.up += 1

    def _down(self):
        lo = self.down * FF_BLOCK
        y = jnp.dot(self.hidden.pop(0), self.w2_ref[lo:lo + FF_BLOCK, :],
                    preferred_element_type=F32)
        self.acc = y if self.acc is None else self.acc + y
        self.down += 1

    def step(self):
        n_blocks = D_FF // FF_BLOCK
        if self.up < n_blocks and self.up <= self.down + 1:
            self._up()
        elif self.down < n_blocks:
            self._down()

    def fill(self, n):
        for _ in range(n):
            self.step()

    def finish(self):
        while self.down * FF_BLOCK < D_FF:
            self.step()
        return _rms(self.x + self.gate * self.acc) * self.g_ref[...]


def _layer_kernel(tiles_per_seq, x_ref, mod_a_ref, mod_b_ref, w_in_ref, w2_ref, b2_ref, gla_g_ref,
                  lb_ref, hgrn_g_ref, w_out_ref, w_mlp1_ref, w_mlp2_ref, final_g_ref, o_ref,
                  s_gla_ref, s_hgrn_ref, x1_ref):
    i = pl.program_id(0)

    @pl.when(i == 0)
    def _():
        x1_ref[...] = jnp.zeros_like(x1_ref)

    @pl.when(i % tiles_per_seq == 0)
    def _():
        s_gla_ref[...] = jnp.zeros_like(s_gla_ref)
        s_hgrn_ref[...] = jnp.zeros_like(s_hgrn_ref)

    mlp = _ChannelMixer(x1_ref[(i + 1) % 2], mod_b_ref, w_mlp1_ref, w_mlp2_ref, final_g_ref)

    def finish():
        o_ref[0] = mlp.finish()

    n_sub = TILE // MIX_ROWS
    for sub in range(n_sub):
        rows = pl.ds(sub * MIX_ROWS, MIX_ROWS)
        x1_ref[i % 2, rows, :] = _token_mixer(
            x_ref[0, rows, :], mod_a_ref, w_in_ref, w2_ref, b2_ref, gla_g_ref, lb_ref, hgrn_g_ref,
            w_out_ref, s_gla_ref, s_hgrn_ref, mlp.fill, finish if sub == n_sub - 1 else lambda: None)


def _const_spec(shape):
    zeros = (0,) * len(shape)
    return pl.BlockSpec(shape, lambda *_: zeros, pipeline_mode=pl.Buffered(1))


def _layer(x, mod, w_in_p, w2_p, b2, gla_g, lb_logits, hgrn_g, w_out_b, w_mlp1_b, w_mlp2_b,
           final_g):
    batch, seq, _ = x.shape
    tps = seq // TILE
    n_tiles = batch * tps

    def tile_in(i):
        j = jnp.minimum(i, n_tiles - 1)
        return (j // tps, j % tps, 0)

    def tile_out(i):
        j = jnp.maximum(i - 1, 0)
        return (j // tps, j % tps, 0)

    return pl.pallas_call(
        functools.partial(_layer_kernel, tps),
        grid=(n_tiles + 1,),
        in_specs=[
            pl.BlockSpec((1, TILE, D_MODEL), tile_in),
            pl.BlockSpec((1, N_MOD, D_MODEL), lambda i: (tile_in(i)[0], 0, 0)),
            pl.BlockSpec((1, N_MOD, D_MODEL), lambda i: (tile_out(i)[0], 0, 0)),
            _const_spec(w_in_p.shape),
            _const_spec(w2_p.shape),
            _const_spec(b2.shape),
            _const_spec(gla_g.shape),
            _const_spec(lb_logits.shape),
            _const_spec(hgrn_g.shape),
            _const_spec(w_out_b.shape),
            _const_spec(w_mlp1_b.shape),
            _const_spec(w_mlp2_b.shape),
            _const_spec(final_g.shape),
        ],
        out_specs=pl.BlockSpec((1, TILE, D_MODEL), tile_out),
        out_shape=jax.ShapeDtypeStruct(x.shape, F32),
        scratch_shapes=[
            pltpu.VMEM((GLA_HEADS, GLA_DK, GLA_DV), F32),
            pltpu.VMEM((HGRN_HEADS, HGRN_DK, HGRN_DV), F32),
            pltpu.VMEM((2, TILE, D_MODEL), F32),
        ],
        compiler_params=pltpu.CompilerParams(
            dimension_semantics=("arbitrary",), vmem_limit_bytes=VMEM_LIMIT),
        name="hybrid_layer",
    )(x, mod, mod, w_in_p, w2_p, b2, gla_g, lb_logits, hgrn_g, w_out_b, w_mlp1_b, w_mlp2_b,
      final_g)


def _permute_w_in(w_in):
    sizes = (GLA_QK, GLA_QK, GLA_V, GLA_GATE_RANK, GLA_V, HGRN_K, HGRN_K, HGRN_V, HGRN_V)
    offs = [0]
    for s in sizes:
        offs.append(offs[-1] + s)
    gq, gk, gv, g_lr, g_gate, hq, hf, hi, h_gate = [w_in[:, offs[i]:offs[i + 1]] for i in range(9)]
    pad = jnp.zeros((w_in.shape[0], RANK_PAD - GLA_GATE_RANK), w_in.dtype)
    return jnp.concatenate([gq, gk, gv, g_gate, hq, hf, hi, h_gate, g_lr, pad], axis=1)


def kernel(x, c, w_ada, b_ada, w_in, gla_gate_w2, gla_gate_b2, gla_norm_g, hgrn_lb_logits,
           hgrn_norm_g, w_out, w_mlp1, w_mlp2, final_norm_g):
    assert w_ada.shape[0] == 1, "single-layer trunk"
    batch = x.shape[0]
    mod = _modulation(c, w_ada[0], b_ada[0]).reshape(batch, N_MOD, D_MODEL)

    w_in_p = _permute_w_in(w_in[0]).astype(BF16)
    w2_p = jnp.concatenate(
        [gla_gate_w2[0], jnp.zeros((RANK_PAD - GLA_GATE_RANK, GLA_QK), F32)], axis=0).astype(BF16)
    return _layer(x, mod, w_in_p, w2_p, gla_gate_b2[0].reshape(1, GLA_QK),
                  gla_norm_g[0].reshape(1, GLA_DV), hgrn_lb_logits,
                  hgrn_norm_g[0].reshape(1, HGRN_DV), w_out[0].astype(BF16),
                  w_mlp1[0].astype(BF16), w_mlp2[0].astype(BF16),
                  final_norm_g.reshape(1, D_MODEL))
```

```python
import functools

import jax
import jax.numpy as jnp
from jax import lax
from jax.experimental import pallas as pl
from jax.experimental.pallas import tpu as pltpu

F32 = jnp.float32
BF16 = jnp.bfloat16

D_MODEL = 1024
GLA_HEADS = 4
GLA_DK = 64
GLA_DV = 128
GLA_QK = GLA_HEADS * GLA_DK
GLA_V = GLA_HEADS * GLA_DV
GLA_GATE_RANK = 16
GLA_GATE_NORMALIZER = 16.0
HGRN_HEADS = 4
HGRN_DK = 128
HGRN_DV = 128
HGRN_K = HGRN_HEADS * HGRN_DK
HGRN_V = HGRN_HEADS * HGRN_DV
CHUNK = 64
D_FF = 4 * D_MODEL
N_MOD = 6
EPS = 1e-6

LANES = 128
RANK_PAD = LANES

OFF_GQ = 0
OFF_GK = OFF_GQ + GLA_QK
OFF_GV = OFF_GK + GLA_QK
OFF_GG = OFF_GV + GLA_V
OFF_HQ = OFF_GG + GLA_V
OFF_HF = OFF_HQ + HGRN_K
OFF_HI = OFF_HF + HGRN_K
OFF_HG = OFF_HI + HGRN_V
OFF_LR = OFF_HG + HGRN_V
IN_WIDTH_P = OFF_LR + RANK_PAD

MIX_ROWS = 256
TILE = 2 * MIX_ROWS
FF_BLOCK = 512
VMEM_LIMIT = 56 * 1024 * 1024


def _rms(x):
    return x * lax.rsqrt(jnp.mean(x * x, axis=-1, keepdims=True) + EPS)


def _sigmoid(x):
    return 1.0 / (1.0 + jnp.exp(-x))


def _silu(x):
    return x * _sigmoid(x)


def _log_sigmoid(x):
    return jnp.minimum(x, 0.0) - jnp.log(1.0 + jnp.exp(-jnp.abs(x)))


def _mod_kernel(c_ref, w_ref, b_ref, o_ref):
    cond = _silu(c_ref[...])
    o_ref[...] = jnp.dot(cond.astype(BF16), w_ref[...].astype(BF16),
                         preferred_element_type=F32) + b_ref[...]


def _modulation(c, w_ada, b_ada):
    batch = c.shape[0]
    n_out = w_ada.shape[1]
    blk = D_MODEL
    return pl.pallas_call(
        _mod_kernel,
        grid=(n_out // blk,),
        in_specs=[
            pl.BlockSpec((batch, D_MODEL), lambda j: (0, 0)),
            pl.BlockSpec((D_MODEL, blk), lambda j: (0, j)),
            pl.BlockSpec((1, blk), lambda j: (0, j)),
        ],
        out_specs=pl.BlockSpec((batch, blk), lambda j: (0, j)),
        out_shape=jax.ShapeDtypeStruct((batch, n_out), F32),
        name="adaln_mod",
    )(c, w_ada, b_ada.reshape(1, n_out))


def _chunk_cumsum(x):
    row_in_chunk = lax.broadcasted_iota(jnp.int32, x.shape, 0) % CHUNK
    s = 1
    while s < CHUNK:
        x = x + jnp.where(row_in_chunk >= s, pltpu.roll(x, s, 0), 0.0)
        s *= 2
    return x


def _chunk_row_bcast(x, r):
    n_chunks = x.shape[0] // CHUNK
    parts = [jnp.broadcast_to(x[j * CHUNK + r:j * CHUNK + r + 1, :], (CHUNK, x.shape[1]))
             for j in range(n_chunks)]
    return jnp.concatenate(parts, axis=0)


def _gated_linear_attention(q, k, v, log_a, s_ref, n_heads, dk, dv, causal, fill):
    tile = q.shape[0]
    n_chunks = tile // CHUNK
    width = n_heads * dk
    col_chunk = lax.broadcasted_iota(jnp.int32, (dk, tile), 1) // CHUNK
    head_of_lane = lax.broadcasted_iota(jnp.int32, (tile, width), 1) // dk

    fill(2 + dk // LANES)
    cum = _chunk_cumsum(log_a)
    cum_mid = _chunk_row_bcast(cum, CHUNK // 2 - 1)
    cum_last = _chunk_row_bcast(cum, CHUNK - 1)
    q_rel = q * jnp.exp(cum - cum_mid)
    k_rel = k * jnp.exp(cum_mid - cum)
    k_end = k * jnp.exp(cum_last - cum)
    q_in = (q * jnp.exp(cum)).astype(BF16)

    k_end_t = k_end.T
    rows8 = lax.broadcasted_iota(jnp.int32, (8, width), 0)
    last8 = jnp.zeros((8, width), F32)
    for j in range(n_chunks):
        last8 = jnp.where(rows8 == j, cum[j * CHUNK + CHUNK - 1:j * CHUNK + CHUNK, :], last8)
    last_pad = jnp.concatenate([last8, jnp.zeros((LANES - 8, width), F32)], axis=0)
    decay_t = jnp.exp(last_pad.T)

    v_b = v.astype(BF16)
    k_rel_b = k_rel.astype(BF16)

    def head_scores(h):
        if dk % LANES == 0:
            return lax.dot_general(
                q_rel[:, h * dk:(h + 1) * dk].astype(BF16), k_rel_b[:, h * dk:(h + 1) * dk],
                (((1,), (1,)), ((), ())), preferred_element_type=F32)
        q_h = jnp.where(head_of_lane == h, q_rel, 0.0).astype(BF16)
        return lax.dot_general(q_h, k_rel_b, (((1,), (1,)), ((), ())),
                               preferred_element_type=F32)

    intra = []
    states = [[] for _ in range(n_chunks)]
    scores_next = head_scores(0)
    for h in range(n_heads):
        v_h = v_b[:, h * dv:(h + 1) * dv]
        scores = scores_next
        if h + 1 < n_heads:
            scores_next = head_scores(h + 1)
        fill(1)
        p = jnp.where(causal, scores, 0.0).astype(BF16)
        intra.append(jnp.dot(p, v_h, preferred_element_type=F32))
        k_t_h = k_end_t[h * dk:(h + 1) * dk, :]
        k_blk = jnp.concatenate(
            [jnp.where(col_chunk == j, k_t_h, 0.0) for j in range(n_chunks)], axis=0)
        upd = jnp.dot(k_blk.astype(BF16), v_h, preferred_element_type=F32)
        s = s_ref[h]
        for j in range(n_chunks):
            states[j].append(s)
            s = decay_t[h * dk:(h + 1) * dk, j:j + 1] * s + upd[j * dk:(j + 1) * dk, :]
        s_ref[h] = s

    inter = []
    for j in range(n_chunks):
        rows = []
        for h in range(n_heads):
            blocks = [states[j][h].astype(BF16) if g == h else jnp.zeros((dk, dv), BF16)
                      for g in range(n_heads)]
            rows.append(jnp.concatenate(blocks, axis=1))
        w_state = jnp.concatenate(rows, axis=0)
        inter.append(jnp.dot(q_in[j * CHUNK:(j + 1) * CHUNK, :], w_state,
                             preferred_element_type=F32))
    return jnp.concatenate(intra, axis=1) + jnp.concatenate(inter, axis=0)


def _head_norm_gate(o, gain, gate, n_heads, dv):
    outs = []
    for h in range(n_heads):
        o_h = o[:, h * dv:(h + 1) * dv]
        outs.append(_rms(o_h) * gain)
    return jnp.concatenate(outs, axis=1) * _silu(gate)


def _token_mixer(x, mod_ref, w_in_ref, w2_ref, b2_ref, gla_g_ref, lb_ref, hgrn_g_ref,
                 s_gla_ref, s_hgrn_ref, fill, after_first_proj):
    tile = x.shape[0]
    shift = mod_ref[0, 0:1, :]
    scale = mod_ref[0, 1:2, :]
    fill(1)
    h_b = (_rms(x) * (1.0 + scale) + shift).astype(BF16)

    def proj(off, width):
        return jnp.dot(h_b, w_in_ref[:, off:off + width], preferred_element_type=F32)

    row = lax.broadcasted_iota(jnp.int32, (tile, tile), 0)
    col = lax.broadcasted_iota(jnp.int32, (tile, tile), 1)
    causal = (row // CHUNK == col // CHUNK) & (col <= row)

    gate_logits = jnp.dot(proj(OFF_LR, RANK_PAD).astype(BF16), w2_ref[...],
                          preferred_element_type=F32) + b2_ref[...]
    gq = proj(OFF_GQ, GLA_QK) * (GLA_DK ** -0.5)
    gk = proj(OFF_GK, GLA_QK)
    gv = proj(OFF_GV, GLA_V)
    gg = proj(OFF_GG, GLA_V)
    after_first_proj()
    log_a = _log_sigmoid(gate_logits) * (1.0 / GLA_GATE_NORMALIZER)
    o_gla = _gated_linear_attention(gq, gk, gv, log_a, s_gla_ref, GLA_HEADS, GLA_DK, GLA_DV,
                                    causal, fill)
    o_gla = _head_norm_gate(o_gla, gla_g_ref[...], gg, GLA_HEADS, GLA_DV)
    fill(1)

    lb_e = jnp.exp(lb_ref[...] - jnp.max(lb_ref[...], axis=0, keepdims=True))
    lb = lb_e[0:1, :] / jnp.sum(lb_e, axis=0, keepdims=True)
    forget = lb + (1.0 - lb) * _sigmoid(proj(OFF_HF, HGRN_K))
    o_hgrn = _gated_linear_attention(
        _silu(proj(OFF_HQ, HGRN_K)), 1.0 - forget, proj(OFF_HI, HGRN_V), jnp.log(forget),
        s_hgrn_ref, HGRN_HEADS, HGRN_DK, HGRN_DV, causal, fill)
    fill(2)
    o_hgrn = _head_norm_gate(o_hgrn, hgrn_g_ref[...], proj(OFF_HG, HGRN_V), HGRN_HEADS, HGRN_DV)
    return jnp.concatenate([o_gla, o_hgrn], axis=1).astype(BF16)


class _ChannelMixer:
    def __init__(self, x, mod_ref, w1_ref, w2_ref):
        self.w1_ref, self.w2_ref = w1_ref, w2_ref
        shift = mod_ref[0, 3:4, :]
        scale = mod_ref[0, 4:5, :]
        self.h_b = (_rms(x) * (1.0 + scale) + shift).astype(BF16)
        self.acc = None
        self.hidden = []
        self.up = 0
        self.down = 0

    def _up(self):
        lo = self.up * FF_BLOCK
        a = jnp.maximum(jnp.dot(self.h_b, self.w1_ref[:, lo:lo + FF_BLOCK],
                                preferred_element_type=F32), 0.0)
        self.hidden.append((a * a).astype(BF16))
        self.up += 1

    def _down(self):
        lo = self.down * FF_BLOCK
        y = jnp.dot(self.hidden.pop(0), self.w2_ref[lo:lo + FF_BLOCK, :],
                    preferred_element_type=F32)
        self.acc = y if self.acc is None else self.acc + y
        self.down += 1

    def step(self):
        n_blocks = D_FF // FF_BLOCK
        if self.up < n_blocks and self.up <= self.down + 1:
            self._up()
        elif self.down < n_blocks:
            self._down()

    def fill(self, n):
        for _ in range(n):
            self.step()

    def result(self):
        while self.down * FF_BLOCK < D_FF:
            self.step()
        return self.acc


def _layer_kernel(n_tiles, tiles_per_seq, x_ref, mod_a_ref, mod_p_ref, w_in_ref, w2_ref, b2_ref,
                  gla_g_ref, lb_ref, hgrn_g_ref, w_out_ref, w_mlp1_ref, w_mlp2_ref, final_g_ref, o_ref,
                  s_gla_ref, s_hgrn_ref, x1a_ref, x1b_ref, mixed_ref, acc_ref):
    j = pl.program_id(0)
    rows_a = pl.ds(0, MIX_ROWS)
    rows_b = pl.ds(MIX_ROWS, MIX_ROWS)

    @pl.when(j == 0)
    def _():
        x1a_ref[...] = jnp.zeros_like(x1a_ref)
        x1b_ref[...] = jnp.zeros_like(x1b_ref)
        mixed_ref[...] = jnp.zeros_like(mixed_ref)
        acc_ref[...] = jnp.zeros_like(acc_ref)

    @pl.when(j % tiles_per_seq == 0)
    def _():
        s_gla_ref[...] = jnp.zeros_like(s_gla_ref)
        s_hgrn_ref[...] = jnp.zeros_like(s_hgrn_ref)

    def deferred_out_proj():
        return jnp.dot(mixed_ref[...], w_out_ref[...], preferred_element_type=F32)

    def finish_channel_mixer(x1_ref, mod_ref, out_rows):
        o_ref[0, out_rows, :] = (
            _rms(x1_ref[...] + mod_ref[0, 5:6, :] * acc_ref[...]) * final_g_ref[...])

    def start_channel_mixer(x1_ref, mod_ref, y):
        x1_ref[...] = x1_ref[...] + mod_ref[0, 2:3, :] * y
        return _ChannelMixer(x1_ref[...], mod_ref, w_mlp1_ref, w_mlp2_ref)

    def unit(x_rows, x1_cur_ref, x1_prev_ref, mod_prev_ref, mod_fin_ref, fin_rows):
        y = deferred_out_proj()
        box = {}

        def fill(n):
            if "mlp" in box:
                box["mlp"].fill(n)

        def after_first_proj():
            finish_channel_mixer(x1_cur_ref, mod_fin_ref, fin_rows)
            box["mlp"] = start_channel_mixer(x1_prev_ref, mod_prev_ref, y)

        x = x_ref[0, x_rows, :]
        mixed = _token_mixer(x, mod_a_ref, w_in_ref, w2_ref, b2_ref, gla_g_ref, lb_ref, hgrn_g_ref,
                             s_gla_ref, s_hgrn_ref, fill, after_first_proj)
        acc = box["mlp"].result()
        mixed_ref[...] = mixed
        x1_cur_ref[...] = x
        acc_ref[...] = acc

    @pl.when(j < n_tiles)
    def _():
        unit(rows_a, x1a_ref, x1b_ref, mod_p_ref, mod_p_ref, rows_a)
        unit(rows_b, x1b_ref, x1a_ref, mod_a_ref, mod_p_ref, rows_b)

    @pl.when(j == n_tiles)
    def _():
        y = deferred_out_proj()
        finish_channel_mixer(x1a_ref, mod_p_ref, rows_a)
        acc_ref[...] = start_channel_mixer(x1b_ref, mod_p_ref, y).result()
        finish_channel_mixer(x1b_ref, mod_p_ref, rows_b)


def _const_spec(shape):
    zeros = (0,) * len(shape)
    return pl.BlockSpec(shape, lambda *_: zeros, pipeline_mode=pl.Buffered(1))


def _layer(x, mod, w_in_p, w2_p, b2, gla_g, lb_logits, hgrn_g, w_out_b, w_mlp1_b, w_mlp2_b,
           final_g):
    batch, seq, _ = x.shape
    tps = seq // TILE
    n_tiles = batch * tps

    def tile_of(j, lag):
        return jnp.clip(j - lag, 0, n_tiles - 1)

    def tile_block(lag):
        return lambda j: (tile_of(j, lag) // tps, tile_of(j, lag) % tps, 0)

    def mod_block(lag):
        return lambda j: (tile_of(j, lag) // tps, 0, 0)

    return pl.pallas_call(
        functools.partial(_layer_kernel, n_tiles, tps),
        grid=(n_tiles + 1,),
        in_specs=[
            pl.BlockSpec((1, TILE, D_MODEL), tile_block(0)),
            pl.BlockSpec((1, N_MOD, D_MODEL), mod_block(0)),
            pl.BlockSpec((1, N_MOD, D_MODEL), mod_block(1)),
            _const_spec(w_in_p.shape),
            _const_spec(w2_p.shape),
            _const_spec(b2.shape),
            _const_spec(gla_g.shape),
            _const_spec(lb_logits.shape),
            _const_spec(hgrn_g.shape),
            _const_spec(w_out_b.shape),
            _const_spec(w_mlp1_b.shape),
            _const_spec(w_mlp2_b.shape),
            _const_spec(final_g.shape),
        ],
        out_specs=pl.BlockSpec((1, TILE, D_MODEL), tile_block(1)),
        out_shape=jax.ShapeDtypeStruct(x.shape, F32),
        scratch_shapes=[
            pltpu.VMEM((GLA_HEADS, GLA_DK, GLA_DV), F32),
            pltpu.VMEM((HGRN_HEADS, HGRN_DK, HGRN_DV), F32),
            pltpu.VMEM((MIX_ROWS, D_MODEL), F32),
            pltpu.VMEM((MIX_ROWS, D_MODEL), F32),
            pltpu.VMEM((MIX_ROWS, D_MODEL), BF16),
            pltpu.VMEM((MIX_ROWS, D_MODEL), F32),
        ],
        compiler_params=pltpu.CompilerParams(
            dimension_semantics=("arbitrary",), vmem_limit_bytes=VMEM_LIMIT),
        name="hybrid_layer",
    )(x, mod, mod, w_in_p, w2_p, b2, gla_g, lb_logits, hgrn_g, w_out_b, w_mlp1_b, w_mlp2_b,
      final_g)


def _permute_w_in(w_in):
    sizes = (GLA_QK, GLA_QK, GLA_V, GLA_GATE_RANK, GLA_V, HGRN_K, HGRN_K, HGRN_V, HGRN_V)
    offs = [0]
    for s in sizes:
        offs.append(offs[-1] + s)
    gq, gk, gv, g_lr, g_gate, hq, hf, hi, h_gate = [w_in[:, offs[i]:offs[i + 1]] for i in range(9)]
    pad = jnp.zeros((w_in.shape[0], RANK_PAD - GLA_GATE_RANK), w_in.dtype)
    return jnp.concatenate([gq, gk, gv, g_gate, hq, hf, hi, h_gate, g_lr, pad], axis=1)


def kernel(x, c, w_ada, b_ada, w_in, gla_gate_w2, gla_gate_b2, gla_norm_g, hgrn_lb_logits,
           hgrn_norm_g, w_out, w_mlp1, w_mlp2, final_norm_g):
    assert w_ada.shape[0] == 1, "single-layer trunk"
    batch = x.shape[0]
    mod = _modulation(c, w_ada[0], b_ada[0]).reshape(batch, N_MOD, D_MODEL)

    w_in_p = _permute_w_in(w_in[0]).astype(BF16)
    w2_p = jnp.concatenate(
        [gla_gate_w2[0], jnp.zeros((RANK_PAD - GLA_GATE_RANK, GLA_QK), F32)], axis=0).astype(BF16)
    return _layer(x, mod, w_in_p, w2_p, gla_gate_b2[0].reshape(1, GLA_QK),
                  gla_norm_g[0].reshape(1, GLA_DV), hgrn_lb_logits,
                  hgrn_norm_g[0].reshape(1, HGRN_DV), w_out[0].astype(BF16),
                  w_mlp1[0].astype(BF16), w_mlp2[0].astype(BF16),
                  final_norm_g.reshape(1, D_MODEL))
```

```python
import functools

import jax
import jax.numpy as jnp
from jax import lax
from jax.experimental import pallas as pl
from jax.experimental.pallas import tpu as pltpu

F32 = jnp.float32
BF16 = jnp.bfloat16

D_MODEL = 1024
GLA_HEADS = 4
GLA_DK = 64
GLA_DV = 128
GLA_QK = GLA_HEADS * GLA_DK
GLA_V = GLA_HEADS * GLA_DV
GLA_GATE_RANK = 16
GLA_GATE_NORMALIZER = 16.0
HGRN_HEADS = 4
HGRN_DK = 128
HGRN_DV = 128
HGRN_K = HGRN_HEADS * HGRN_DK
HGRN_V = HGRN_HEADS * HGRN_DV
CHUNK = 64
D_FF = 4 * D_MODEL
N_MOD = 6
EPS = 1e-6

LANES = 128
RANK_PAD = LANES

OFF_GQ = 0
OFF_GK = OFF_GQ + GLA_QK
OFF_GV = OFF_GK + GLA_QK
OFF_GG = OFF_GV + GLA_V
OFF_HQ = OFF_GG + GLA_V
OFF_HF = OFF_HQ + HGRN_K
OFF_HI = OFF_HF + HGRN_K
OFF_HG = OFF_HI + HGRN_V
OFF_LR = OFF_HG + HGRN_V
IN_WIDTH_P = OFF_LR + RANK_PAD

TILE = 512
MIX_ROWS = 256
FF_BLOCK = 256
VMEM_LIMIT = 56 * 1024 * 1024


def _rms(x):
    return x * lax.rsqrt(jnp.mean(x * x, axis=-1, keepdims=True) + EPS)


def _sigmoid(x):
    return 1.0 / (1.0 + jnp.exp(-x))


def _silu(x):
    return x * _sigmoid(x)


def _log_sigmoid(x):
    return jnp.minimum(x, 0.0) - jnp.log(1.0 + jnp.exp(-jnp.abs(x)))


def _mod_kernel(c_ref, w_ref, b_ref, o_ref):
    cond = _silu(c_ref[...])
    o_ref[...] = jnp.dot(cond.astype(BF16), w_ref[...].astype(BF16),
                         preferred_element_type=F32) + b_ref[...]


def _modulation(c, w_ada, b_ada):
    batch = c.shape[0]
    n_out = w_ada.shape[1]
    blk = D_MODEL
    return pl.pallas_call(
        _mod_kernel,
        grid=(n_out // blk,),
        in_specs=[
            pl.BlockSpec((batch, D_MODEL), lambda j: (0, 0)),
            pl.BlockSpec((D_MODEL, blk), lambda j: (0, j)),
            pl.BlockSpec((1, blk), lambda j: (0, j)),
        ],
        out_specs=pl.BlockSpec((batch, blk), lambda j: (0, j)),
        out_shape=jax.ShapeDtypeStruct((batch, n_out), F32),
        name="adaln_mod",
    )(c, w_ada, b_ada.reshape(1, n_out))


def _chunk_scan(x):
    row = lax.broadcasted_iota(jnp.int32, x.shape, 0)
    s = 1
    while s < CHUNK:
        x = x + jnp.where(row >= s, pltpu.roll(x, s, 0), 0.0)
        s *= 2
    return x


def _blocks_to_array(blocks):
    return jnp.concatenate([jnp.concatenate(row, axis=1) for row in blocks], axis=0)


def _block_diag(blocks, n):
    zero = jnp.zeros_like(blocks[0])
    return jnp.concatenate(
        [jnp.concatenate([blocks[h] if g == h else zero for g in range(n)], axis=1)
         for h in range(n)], axis=0)


def _gated_linear_attention(q_fn, k_la_fn, v, s_ref, n_heads, dk, dv, fill):
    tile = v.shape[0]
    n_chunks = tile // CHUNK
    width = n_heads * dk
    n_lane_blocks = width // LANES
    heads_per_lane_block = LANES // dk if dk < LANES else 1
    head_of_lane = lax.broadcasted_iota(jnp.int32, (CHUNK, width), 1) // dk
    key_le_query = (lax.broadcasted_iota(jnp.int32, (CHUNK, n_heads * CHUNK), 1) % CHUNK
                    <= lax.broadcasted_iota(jnp.int32, (CHUNK, n_heads * CHUNK), 0))

    fill(2)
    q_rel = [[None] * n_lane_blocks for _ in range(n_chunks)]
    k_rel = [[None] * n_lane_blocks for _ in range(n_chunks)]
    k_end = [[None] * n_lane_blocks for _ in range(n_chunks)]
    q_in = [[None] * n_lane_blocks for _ in range(n_chunks)]
    last = [[None] * n_lane_blocks for _ in range(n_chunks)]
    for c in range(n_lane_blocks):
        cols = slice(c * LANES, (c + 1) * LANES)
        for j in range(n_chunks):
            rows = slice(j * CHUNK, (j + 1) * CHUNK)
            q = q_fn(rows, cols)
            k, log_a = k_la_fn(rows, cols)
            cum = _chunk_scan(log_a)
            cum_mid = cum[CHUNK // 2 - 1:CHUNK // 2, :]
            cum_last = cum[CHUNK - 1:CHUNK, :]
            q_rel[j][c] = (q * jnp.exp(cum - cum_mid)).astype(BF16)
            k_rel[j][c] = (k * jnp.exp(cum_mid - cum)).astype(BF16)
            k_end[j][c] = k * jnp.exp(cum_last - cum)
            q_in[j][c] = (q * jnp.exp(cum)).astype(BF16)
            last[j][c] = cum_last
        fill(1)

    rows8 = lax.broadcasted_iota(jnp.int32, (8, width), 0)
    last8 = jnp.zeros((8, width), F32)
    for j in range(n_chunks):
        last8 = jnp.where(rows8 == j, jnp.concatenate(last[j], axis=1), last8)
    last_pad = jnp.concatenate([last8, jnp.zeros((LANES - 8, width), F32)], axis=0)
    decay_t = jnp.exp(last_pad.T)

    v_b = v.astype(BF16)

    def chunk_scores(j):
        k_j = jnp.concatenate(k_rel[j], axis=1)
        if dk % LANES == 0:
            per_head = [k_j[:, h * dk:(h + 1) * dk] for h in range(n_heads)]
            k_bd = _block_diag(per_head, n_heads)
        else:
            k_bd = jnp.concatenate(
                [jnp.where(head_of_lane == h, k_j, jnp.zeros_like(k_j)) for h in range(n_heads)],
                axis=0)
        return lax.dot_general(jnp.concatenate(q_rel[j], axis=1), k_bd, (((1,), (1,)), ((), ())),
                               preferred_element_type=F32)

    state = [s_ref[h] for h in range(n_heads)]
    out = []
    scores_next = chunk_scores(0)
    for j in range(n_chunks):
        rows = slice(j * CHUNK, (j + 1) * CHUNK)
        scores = scores_next
        if j + 1 < n_chunks:
            scores_next = chunk_scores(j + 1)
        fill(2)
        v_heads = [v_b[rows, h * dv:(h + 1) * dv] for h in range(n_heads)]
        v_bd = _block_diag(v_heads, n_heads)
        p = jnp.where(key_le_query, scores, 0.0).astype(BF16)
        o_j = jnp.dot(p, v_bd, preferred_element_type=F32)
        w_state = _block_diag([s.astype(BF16) for s in state], n_heads)
        o_j = o_j + jnp.dot(jnp.concatenate(q_in[j], axis=1), w_state, preferred_element_type=F32)
        out.append(o_j)
        if dk % LANES == 0:
            k_t = jnp.concatenate(k_end[j], axis=0).T.astype(BF16)
            upd = jnp.dot(k_t, v_bd, preferred_element_type=F32)
            upd_heads = [upd[:, h * dv:(h + 1) * dv] for h in range(n_heads)]
        else:
            k_t = jnp.concatenate(k_end[j], axis=0).T.astype(BF16)
            zero = jnp.zeros_like(v_heads[0])
            v_sel = jnp.concatenate(
                [jnp.concatenate([v_heads[h] if h // heads_per_lane_block == c else zero
                                  for h in range(n_heads)], axis=1)
                 for c in range(n_lane_blocks)], axis=0)
            upd = jnp.dot(k_t, v_sel, preferred_element_type=F32)
            upd_heads = [upd[(h % heads_per_lane_block) * dk:(h % heads_per_lane_block + 1) * dk,
                             h * dv:(h + 1) * dv] for h in range(n_heads)]
        state = [decay_t[h * dk:(h + 1) * dk, j:j + 1] * state[h] + upd_heads[h]
                 for h in range(n_heads)]
    for h in range(n_heads):
        s_ref[h] = state[h]
    out = jnp.concatenate(out, axis=0)
    return [out[:, h * dv:(h + 1) * dv] for h in range(n_heads)]


def _head_norm_gate(heads, gain, gate, dv):
    outs = [_rms(o_h) * gain * _silu(gate[:, h * dv:(h + 1) * dv]) for h, o_h in enumerate(heads)]
    return jnp.concatenate(outs, axis=1)


def _token_mixer(x, mod_ref, w_in_ref, w2_ref, b2_ref, gla_g_ref, lb_ref, hgrn_g_ref, w_out_ref,
                 s_gla_ref, s_hgrn_ref, fill, finish):
    shift = mod_ref[0, 0:1, :]
    scale = mod_ref[0, 1:2, :]
    gate = mod_ref[0, 2:3, :]
    fill(2)
    h_b = (_rms(x) * (1.0 + scale) + shift).astype(BF16)

    def proj(off, width):
        return jnp.dot(h_b, w_in_ref[:, off:off + width], preferred_element_type=F32)

    gate_logits = jnp.dot(proj(OFF_LR, RANK_PAD).astype(BF16), w2_ref[...],
                          preferred_element_type=F32) + b2_ref[...]
    gq = proj(OFF_GQ, GLA_QK)
    gk = proj(OFF_GK, GLA_QK)
    o_gla = _gated_linear_attention(
        lambda rows, cols: gq[rows, cols] * (GLA_DK ** -0.5),
        lambda rows, cols: (gk[rows, cols],
                            _log_sigmoid(gate_logits[rows, cols]) * (1.0 / GLA_GATE_NORMALIZER)),
        proj(OFF_GV, GLA_V), s_gla_ref, GLA_HEADS, GLA_DK, GLA_DV, fill)
    o_gla = _head_norm_gate(o_gla, gla_g_ref[...], proj(OFF_GG, GLA_V), GLA_DV)
    fill(2)

    lb_e = jnp.exp(lb_ref[...] - jnp.max(lb_ref[...], axis=0, keepdims=True))
    lb = lb_e[0:1, :] / jnp.sum(lb_e, axis=0, keepdims=True)
    hq = proj(OFF_HQ, HGRN_K)
    hf = proj(OFF_HF, HGRN_K)

    def hgrn_k_la(rows, cols):
        forget = lb[:, cols] + (1.0 - lb[:, cols]) * _sigmoid(hf[rows, cols])
        return 1.0 - forget, jnp.log(forget)

    o_hgrn = _gated_linear_attention(
        lambda rows, cols: _silu(hq[rows, cols]), hgrn_k_la, proj(OFF_HI, HGRN_V),
        s_hgrn_ref, HGRN_HEADS, HGRN_DK, HGRN_DV, fill)
    o_hgrn = _head_norm_gate(o_hgrn, hgrn_g_ref[...], proj(OFF_HG, HGRN_V), HGRN_DV)
    finish()

    mixed = jnp.concatenate([o_gla, o_hgrn], axis=1).astype(BF16)
    return x + gate * jnp.dot(mixed, w_out_ref[...], preferred_element_type=F32)


class _ChannelMixer:
    def __init__(self, x, mod_ref, w1_ref, w2_ref, g_ref):
        self.x, self.w1_ref, self.w2_ref, self.g_ref = x, w1_ref, w2_ref, g_ref
        shift = mod_ref[0, 3:4, :]
        scale = mod_ref[0, 4:5, :]
        self.gate = mod_ref[0, 5:6, :]
        self.h_b = (_rms(x) * (1.0 + scale) + shift).astype(BF16)
        self.acc = None
        self.hidden = []
        self.up = 0
        self.down = 0

    def _up(self):
        lo = self.up * FF_BLOCK
        a = jnp.maximum(jnp.dot(self.h_b, self.w1_ref[:, lo:lo + FF_BLOCK],
                                preferred_element_type=F32), 0.0)
        self.hidden.append((a * a).astype(BF16))
        self.up += 1

    def _down(self):
        lo = self.down * FF_BLOCK
        y = jnp.dot(self.hidden.pop(0), self.w2_ref[lo:lo + FF_BLOCK, :],
                    preferred_element_type=F32)
        self.acc = y if self.acc is None else self.acc + y
        self.down += 1

    def step(self):
        n_blocks = D_FF // FF_BLOCK
        if self.up < n_blocks and self.up <= self.down + 1:
            self._up()
        elif self.down < n_blocks:
            self._down()

    def fill(self, n):
        for _ in range(n):
            self.step()

    def finish(self):
        while self.down * FF_BLOCK < D_FF:
            self.step()
        return _rms(self.x + self.gate * self.acc) * self.g_ref[...]


def _layer_kernel(tiles_per_seq, x_ref, mod_a_ref, mod_b_ref, w_in_ref, w2_ref, b2_ref, gla_g_ref,
                  lb_ref, hgrn_g_ref, w_out_ref, w_mlp1_ref, w_mlp2_ref, final_g_ref, o_ref,
                  s_gla_ref, s_hgrn_ref, x1_ref):
    i = pl.program_id(0)

    @pl.when(i == 0)
    def _():
        x1_ref[...] = jnp.zeros_like(x1_ref)

    @pl.when(i % tiles_per_seq == 0)
    def _():
        s_gla_ref[...] = jnp.zeros_like(s_gla_ref)
        s_hgrn_ref[...] = jnp.zeros_like(s_hgrn_ref)

    mlp = _ChannelMixer(x1_ref[(i + 1) % 2], mod_b_ref, w_mlp1_ref, w_mlp2_ref, final_g_ref)

    def finish():
        o_ref[0] = mlp.finish()

    n_sub = TILE // MIX_ROWS
    for sub in range(n_sub):
        rows = pl.ds(sub * MIX_ROWS, MIX_ROWS)
        x1_ref[i % 2, rows, :] = _token_mixer(
            x_ref[0, rows, :], mod_a_ref, w_in_ref, w2_ref, b2_ref, gla_g_ref, lb_ref, hgrn_g_ref,
            w_out_ref, s_gla_ref, s_hgrn_ref, mlp.fill, finish if sub == n_sub - 1 else lambda: None)


def _const_spec(shape):
    zeros = (0,) * len(shape)
    return pl.BlockSpec(shape, lambda *_: zeros, pipeline_mode=pl.Buffered(1))


def _layer(x, mod, w_in_p, w2_p, b2, gla_g, lb_logits, hgrn_g, w_out_b, w_mlp1_b, w_mlp2_b,
           final_g):
    batch, seq, _ = x.shape
    tps = seq // TILE
    n_tiles = batch * tps

    def tile_in(i):
        j = jnp.minimum(i, n_tiles - 1)
        return (j // tps, j % tps, 0)

    def tile_out(i):
        j = jnp.maximum(i - 1, 0)
        return (j // tps, j % tps, 0)

    return pl.pallas_call(
        functools.partial(_layer_kernel, tps),
        grid=(n_tiles + 1,),
        in_specs=[
            pl.BlockSpec((1, TILE, D_MODEL), tile_in),
            pl.BlockSpec((1, N_MOD, D_MODEL), lambda i: (tile_in(i)[0], 0, 0)),
            pl.BlockSpec((1, N_MOD, D_MODEL), lambda i: (tile_out(i)[0], 0, 0)),
            _const_spec(w_in_p.shape),
            _const_spec(w2_p.shape),
            _const_spec(b2.shape),
            _const_spec(gla_g.shape),
            _const_spec(lb_logits.shape),
            _const_spec(hgrn_g.shape),
            _const_spec(w_out_b.shape),
            _const_spec(w_mlp1_b.shape),
            _const_spec(w_mlp2_b.shape),
            _const_spec(final_g.shape),
        ],
        out_specs=pl.BlockSpec((1, TILE, D_MODEL), tile_out),
        out_shape=jax.ShapeDtypeStruct(x.shape, F32),
        scratch_shapes=[
            pltpu.VMEM((GLA_HEADS, GLA_DK, GLA_DV), F32),
            pltpu.VMEM((HGRN_HEADS, HGRN_DK, HGRN_DV), F32),
            pltpu.VMEM((2, TILE, D_MODEL), F32),
        ],
        compiler_params=pltpu.CompilerParams(
            dimension_semantics=("arbitrary",), vmem_limit_bytes=VMEM_LIMIT),
        name="hybrid_layer",
    )(x, mod, mod, w_in_p, w2_p, b2, gla_g, lb_logits, hgrn_g, w_out_b, w_mlp1_b, w_mlp2_b,
      final_g)


def _permute_w_in(w_in):
    sizes = (GLA_QK, GLA_QK, GLA_V, GLA_GATE_RANK, GLA_V, HGRN_K, HGRN_K, HGRN_V, HGRN_V)
    offs = [0]
    for s in sizes:
        offs.append(offs[-1] + s)
    gq, gk, gv, g_lr, g_gate, hq, hf, hi, h_gate = [w_in[:, offs[i]:offs[i + 1]] for i in range(9)]
    pad = jnp.zeros((w_in.shape[0], RANK_PAD - GLA_GATE_RANK), w_in.dtype)
    return jnp.concatenate([gq, gk, gv, g_gate, hq, hf, hi, h_gate, g_lr, pad], axis=1)


def kernel(x, c, w_ada, b_ada, w_in, gla_gate_w2, gla_gate_b2, gla_norm_g, hgrn_lb_logits,
           hgrn_norm_g, w_out, w_mlp1, w_mlp2, final_norm_g):
    assert w_ada.shape[0] == 1, "single-layer trunk"
    batch = x.shape[0]
    mod = _modulation(c, w_ada[0], b_ada[0]).reshape(batch, N_MOD, D_MODEL)

    w_in_p = _permute_w_in(w_in[0]).astype(BF16)
    w2_p = jnp.concatenate(
        [gla_gate_w2[0], jnp.zeros((RANK_PAD - GLA_GATE_RANK, GLA_QK), F32)], axis=0).astype(BF16)
    return _layer(x, mod, w_in_p, w2_p, gla_gate_b2[0].reshape(1, GLA_QK),
                  gla_norm_g[0].reshape(1, GLA_DV), hgrn_lb_logits,
                  hgrn_norm_g[0].reshape(1, HGRN_DV), w_out[0].astype(BF16),
                  w_mlp1[0].astype(BF16), w_mlp2[0].astype(BF16),
                  final_norm_g.reshape(1, D_MODEL))
```

```python
import functools

import jax
import jax.numpy as jnp
from jax import lax
from jax.experimental import pallas as pl
from jax.experimental.pallas import tpu as pltpu

F32 = jnp.float32
BF16 = jnp.bfloat16

D_MODEL = 1024
GLA_HEADS = 4
GLA_DK = 64
GLA_DV = 128
GLA_QK = GLA_HEADS * GLA_DK
GLA_V = GLA_HEADS * GLA_DV
GLA_GATE_RANK = 16
GLA_GATE_NORMALIZER = 16.0
HGRN_HEADS = 4
HGRN_DK = 128
HGRN_DV = 128
HGRN_K = HGRN_HEADS * HGRN_DK
HGRN_V = HGRN_HEADS * HGRN_DV
CHUNK = 64
D_FF = 4 * D_MODEL
N_MOD = 6
EPS = 1e-6

LANES = 128
RANK_PAD = LANES

OFF_GQ = 0
OFF_GK = OFF_GQ + GLA_QK
OFF_GV = OFF_GK + GLA_QK
OFF_GG = OFF_GV + GLA_V
OFF_HQ = OFF_GG + GLA_V
OFF_HF = OFF_HQ + HGRN_K
OFF_HI = OFF_HF + HGRN_K
OFF_HG = OFF_HI + HGRN_V
OFF_LR = OFF_HG + HGRN_V
IN_WIDTH_P = OFF_LR + RANK_PAD

TILE = 512
MIX_ROWS = 256
FF_BLOCK = 256
FILL_UNITS_PER_MIX = 30
VMEM_LIMIT = 56 * 1024 * 1024


def _rms(x):
    return x * lax.rsqrt(jnp.mean(x * x, axis=-1, keepdims=True) + EPS)


def _sigmoid(x):
    return 1.0 / (1.0 + jnp.exp(-x))


def _silu(x):
    return x * _sigmoid(x)


def _log_sigmoid(x):
    return jnp.minimum(x, 0.0) - jnp.log(1.0 + jnp.exp(-jnp.abs(x)))


def _mod_kernel(c_ref, w_ref, b_ref, o_ref):
    cond = _silu(c_ref[...])
    o_ref[...] = jnp.dot(cond.astype(BF16), w_ref[...].astype(BF16),
                         preferred_element_type=F32) + b_ref[...]


def _modulation(c, w_ada, b_ada):
    batch = c.shape[0]
    n_out = w_ada.shape[1]
    blk = D_MODEL
    return pl.pallas_call(
        _mod_kernel,
        grid=(n_out // blk,),
        in_specs=[
            pl.BlockSpec((batch, D_MODEL), lambda j: (0, 0)),
            pl.BlockSpec((D_MODEL, blk), lambda j: (0, j)),
            pl.BlockSpec((1, blk), lambda j: (0, j)),
        ],
        out_specs=pl.BlockSpec((batch, blk), lambda j: (0, j)),
        out_shape=jax.ShapeDtypeStruct((batch, n_out), F32),
        name="adaln_mod",
    )(c, w_ada, b_ada.reshape(1, n_out))


def _chunk_scan(x):
    row = lax.broadcasted_iota(jnp.int32, x.shape, 0)
    s = 1
    while s < CHUNK:
        x = x + jnp.where(row >= s, pltpu.roll(x, s, 0), 0.0)
        s *= 2
    return x


def _blocks_to_array(blocks):
    return jnp.concatenate([jnp.concatenate(row, axis=1) for row in blocks], axis=0)


def _block_diag(blocks, n):
    zero = jnp.zeros_like(blocks[0])
    return jnp.concatenate(
        [jnp.concatenate([blocks[h] if g == h else zero for g in range(n)], axis=1)
         for h in range(n)], axis=0)


def _gated_linear_attention(q_fn, k_la_fn, v, s_ref, n_heads, dk, dv, fill):
    tile = v.shape[0]
    n_chunks = tile // CHUNK
    width = n_heads * dk
    n_lane_blocks = width // LANES
    heads_per_lane_block = LANES // dk if dk < LANES else 1
    head_of_lane = lax.broadcasted_iota(jnp.int32, (CHUNK, width), 1) // dk
    key_le_query = (lax.broadcasted_iota(jnp.int32, (CHUNK, n_heads * CHUNK), 1) % CHUNK
                    <= lax.broadcasted_iota(jnp.int32, (CHUNK, n_heads * CHUNK), 0))

    fill(2)
    q_rel = [[None] * n_lane_blocks for _ in range(n_chunks)]
    k_rel = [[None] * n_lane_blocks for _ in range(n_chunks)]
    k_end = [[None] * n_lane_blocks for _ in range(n_chunks)]
    q_in = [[None] * n_lane_blocks for _ in range(n_chunks)]
    last = [[None] * n_lane_blocks for _ in range(n_chunks)]
    for c in range(n_lane_blocks):
        cols = slice(c * LANES, (c + 1) * LANES)
        for j in range(n_chunks):
            rows = slice(j * CHUNK, (j + 1) * CHUNK)
            q = q_fn(rows, cols)
            k, log_a = k_la_fn(rows, cols)
            cum = _chunk_scan(log_a)
            cum_mid = cum[CHUNK // 2 - 1:CHUNK // 2, :]
            cum_last = cum[CHUNK - 1:CHUNK, :]
            q_rel[j][c] = (q * jnp.exp(cum - cum_mid)).astype(BF16)
            k_rel[j][c] = (k * jnp.exp(cum_mid - cum)).astype(BF16)
            k_end[j][c] = k * jnp.exp(cum_last - cum)
            q_in[j][c] = (q * jnp.exp(cum)).astype(BF16)
            last[j][c] = cum_last
        fill(1)

    rows8 = lax.broadcasted_iota(jnp.int32, (8, width), 0)
    last8 = jnp.zeros((8, width), F32)
    for j in range(n_chunks):
        last8 = jnp.where(rows8 == j, jnp.concatenate(last[j], axis=1), last8)
    last_pad = jnp.concatenate([last8, jnp.zeros((LANES - 8, width), F32)], axis=0)
    decay_t = jnp.exp(last_pad.T)

    v_b = v.astype(BF16)

    def chunk_scores(j):
        k_j = jnp.concatenate(k_rel[j], axis=1)
        if dk % LANES == 0:
            per_head = [k_j[:, h * dk:(h + 1) * dk] for h in range(n_heads)]
            k_bd = _block_diag(per_head, n_heads)
        else:
            k_bd = jnp.concatenate(
                [jnp.where(head_of_lane == h, k_j, jnp.zeros_like(k_j)) for h in range(n_heads)],
                axis=0)
        return lax.dot_general(jnp.concatenate(q_rel[j], axis=1), k_bd, (((1,), (1,)), ((), ())),
                               preferred_element_type=F32)

    state = [s_ref[h] for h in range(n_heads)]
    out = []
    scores_next = chunk_scores(0)
    for j in range(n_chunks):
        rows = slice(j * CHUNK, (j + 1) * CHUNK)
        scores = scores_next
        if j + 1 < n_chunks:
            scores_next = chunk_scores(j + 1)
        fill(2)
        v_heads = [v_b[rows, h * dv:(h + 1) * dv] for h in range(n_heads)]
        v_bd = _block_diag(v_heads, n_heads)
        p = jnp.where(key_le_query, scores, 0.0).astype(BF16)
        o_j = jnp.dot(p, v_bd, preferred_element_type=F32)
        w_state = _block_diag([s.astype(BF16) for s in state], n_heads)
        o_j = o_j + jnp.dot(jnp.concatenate(q_in[j], axis=1), w_state, preferred_element_type=F32)
        out.append(o_j)
        if dk % LANES == 0:
            k_t = jnp.concatenate(k_end[j], axis=0).T.astype(BF16)
            upd = jnp.dot(k_t, v_bd, preferred_element_type=F32)
            upd_heads = [upd[:, h * dv:(h + 1) * dv] for h in range(n_heads)]
        else:
            k_t = jnp.concatenate(k_end[j], axis=0).T.astype(BF16)
            zero = jnp.zeros_like(v_heads[0])
            v_sel = jnp.concatenate(
                [jnp.concatenate([v_heads[h] if h // heads_per_lane_block == c else zero
                                  for h in range(n_heads)], axis=1)
                 for c in range(n_lane_blocks)], axis=0)
            upd = jnp.dot(k_t, v_sel, preferred_element_type=F32)
            upd_heads = [upd[(h % heads_per_lane_block) * dk:(h % heads_per_lane_block + 1) * dk,
                             h * dv:(h + 1) * dv] for h in range(n_heads)]
        state = [decay_t[h * dk:(h + 1) * dk, j:j + 1] * state[h] + upd_heads[h]
                 for h in range(n_heads)]
    for h in range(n_heads):
        s_ref[h] = state[h]
    out = jnp.concatenate(out, axis=0)
    return [out[:, h * dv:(h + 1) * dv] for h in range(n_heads)]


def _head_norm_gate(heads, gain, gate, dv):
    outs = [_rms(o_h) * gain * _silu(gate[:, h * dv:(h + 1) * dv]) for h, o_h in enumerate(heads)]
    return jnp.concatenate(outs, axis=1)


def _token_mixer(x, mod_ref, w_in_refs, w2_ref, b2_ref, gla_g_ref, lb_ref, hgrn_g_ref, w_out_ref,
                 s_gla_ref, s_hgrn_ref, fill, finish):
    shift = mod_ref[0, 0:1, :]
    scale = mod_ref[0, 1:2, :]
    gate = mod_ref[0, 2:3, :]
    fill(2)
    h_b = (_rms(x) * (1.0 + scale) + shift).astype(BF16)

    def proj(off, width):
        for ref in w_in_refs:
            if off < ref.shape[1]:
                return jnp.dot(h_b, ref[:, off:off + width], preferred_element_type=F32)
            off -= ref.shape[1]

    gate_logits = jnp.dot(proj(OFF_LR, RANK_PAD).astype(BF16), w2_ref[...],
                          preferred_element_type=F32) + b2_ref[...]
    gq = proj(OFF_GQ, GLA_QK)
    gk = proj(OFF_GK, GLA_QK)
    o_gla = _gated_linear_attention(
        lambda rows, cols: gq[rows, cols] * (GLA_DK ** -0.5),
        lambda rows, cols: (gk[rows, cols],
                            _log_sigmoid(gate_logits[rows, cols]) * (1.0 / GLA_GATE_NORMALIZER)),
        proj(OFF_GV, GLA_V), s_gla_ref, GLA_HEADS, GLA_DK, GLA_DV, fill)
    o_gla = _head_norm_gate(o_gla, gla_g_ref[...], proj(OFF_GG, GLA_V), GLA_DV)
    fill(2)

    lb_e = jnp.exp(lb_ref[...] - jnp.max(lb_ref[...], axis=0, keepdims=True))
    lb = lb_e[0:1, :] / jnp.sum(lb_e, axis=0, keepdims=True)
    hq = proj(OFF_HQ, HGRN_K)
    hf = proj(OFF_HF, HGRN_K)

    def hgrn_k_la(rows, cols):
        forget = lb[:, cols] + (1.0 - lb[:, cols]) * _sigmoid(hf[rows, cols])
        return 1.0 - forget, jnp.log(forget)

    o_hgrn = _gated_linear_attention(
        lambda rows, cols: _silu(hq[rows, cols]), hgrn_k_la, proj(OFF_HI, HGRN_V),
        s_hgrn_ref, HGRN_HEADS, HGRN_DK, HGRN_DV, fill)
    o_hgrn = _head_norm_gate(o_hgrn, hgrn_g_ref[...], proj(OFF_HG, HGRN_V), HGRN_DV)
    finish()

    mixed = jnp.concatenate([o_gla, o_hgrn], axis=1).astype(BF16)
    return x + gate * jnp.dot(mixed, w_out_ref[...], preferred_element_type=F32)


class _ChannelMixer:
    def __init__(self, x, mod_ref, w1_ref, w2_ref, g_ref):
        self.x, self.w1_ref, self.w2_ref, self.g_ref = x, w1_ref, w2_ref, g_ref
        shift = mod_ref[0, 3:4, :]
        scale = mod_ref[0, 4:5, :]
        self.gate = mod_ref[0, 5:6, :]
        self.h_b = (_rms(x) * (1.0 + scale) + shift).astype(BF16)
        self.acc = None
        self.hidden = []
        self.credit = 0.0
        self.up = 0
        self.down = 0

    def _up(self):
        lo = self.up * FF_BLOCK
        a = jnp.maximum(jnp.dot(self.h_b, self.w1_ref[:, lo:lo + FF_BLOCK],
                                preferred_element_type=F32), 0.0)
        self.hidden.append((a * a).astype(BF16))
        self.up += 1

    def _down(self):
        lo = self.down * FF_BLOCK
        y = jnp.dot(self.hidden.pop(0), self.w2_ref[lo:lo + FF_BLOCK, :],
                    preferred_element_type=F32)
        self.acc = y if self.acc is None else self.acc + y
        self.down += 1

    def step(self):
        n_blocks = D_FF // FF_BLOCK
        if self.up < n_blocks and self.up <= self.down + 1:
            self._up()
        elif self.down < n_blocks:
            self._down()

    def fill(self, n):
        self.credit += n * (2 * D_FF // FF_BLOCK) / (FILL_UNITS_PER_MIX * (TILE // MIX_ROWS))
        while self.credit >= 1.0:
            self.step()
            self.credit -= 1.0

    def finish(self):
        while self.down * FF_BLOCK < D_FF:
            self.step()
        return _rms(self.x + self.gate * self.acc) * self.g_ref[...]


def _layer_kernel(tiles_per_seq, x_ref, mod_a_ref, mod_b_ref, w_in_a_ref, w_in_b_ref, w_in_lr_ref,
                  w2_ref, b2_ref, gla_g_ref, lb_ref, hgrn_g_ref, w_out_ref, w_mlp1_ref, w_mlp2_ref,
                  final_g_ref, o_ref,
                  s_gla_ref, s_hgrn_ref, x1_ref):
    i = pl.program_id(0)

    @pl.when(i == 0)
    def _():
        x1_ref[...] = jnp.zeros_like(x1_ref)

    @pl.when(i % tiles_per_seq == 0)
    def _():
        s_gla_ref[...] = jnp.zeros_like(s_gla_ref)
        s_hgrn_ref[...] = jnp.zeros_like(s_hgrn_ref)

    mlp = _ChannelMixer(x1_ref[(i + 1) % 2], mod_b_ref, w_mlp1_ref, w_mlp2_ref, final_g_ref)

    def finish():
        o_ref[0] = mlp.finish()

    n_sub = TILE // MIX_ROWS
    for sub in range(n_sub):
        rows = pl.ds(sub * MIX_ROWS, MIX_ROWS)
        x1_ref[i % 2, rows, :] = _token_mixer(
            x_ref[0, rows, :], mod_a_ref, (w_in_a_ref, w_in_b_ref, w_in_lr_ref), w2_ref, b2_ref,
            gla_g_ref, lb_ref, hgrn_g_ref,
            w_out_ref, s_gla_ref, s_hgrn_ref, mlp.fill, finish if sub == n_sub - 1 else lambda: None)


def _const_spec(shape):
    zeros = (0,) * len(shape)
    return pl.BlockSpec(shape, lambda *_: zeros, pipeline_mode=pl.Buffered(1))


def _layer(x, mod, w_in_parts, w2_p, b2, gla_g, lb_logits, hgrn_g, w_out_b, w_mlp1_b, w_mlp2_b,
           final_g):
    batch, seq, _ = x.shape
    tps = seq // TILE
    n_tiles = batch * tps

    def tile_in(i):
        j = jnp.minimum(i, n_tiles - 1)
        return (j // tps, j % tps, 0)

    def tile_out(i):
        j = jnp.maximum(i - 1, 0)
        return (j // tps, j % tps, 0)

    return pl.pallas_call(
        functools.partial(_layer_kernel, tps),
        grid=(n_tiles + 1,),
        in_specs=[
            pl.BlockSpec((1, TILE, D_MODEL), tile_in),
            pl.BlockSpec((1, N_MOD, D_MODEL), lambda i: (tile_in(i)[0], 0, 0)),
            pl.BlockSpec((1, N_MOD, D_MODEL), lambda i: (tile_out(i)[0], 0, 0)),
            *[_const_spec(w.shape) for w in w_in_parts],
            _const_spec(w2_p.shape),
            _const_spec(b2.shape),
            _const_spec(gla_g.shape),
            _const_spec(lb_logits.shape),
            _const_spec(hgrn_g.shape),
            _const_spec(w_out_b.shape),
            _const_spec(w_mlp1_b.shape),
            _const_spec(w_mlp2_b.shape),
            _const_spec(final_g.shape),
        ],
        out_specs=pl.BlockSpec((1, TILE, D_MODEL), tile_out),
        out_shape=jax.ShapeDtypeStruct(x.shape, F32),
        scratch_shapes=[
            pltpu.VMEM((GLA_HEADS, GLA_DK, GLA_DV), F32),
            pltpu.VMEM((HGRN_HEADS, HGRN_DK, HGRN_DV), F32),
            pltpu.VMEM((2, TILE, D_MODEL), F32),
        ],
        compiler_params=pltpu.CompilerParams(
            dimension_semantics=("arbitrary",), vmem_limit_bytes=VMEM_LIMIT),
        name="hybrid_layer",
    )(x, mod, mod, *w_in_parts, w2_p, b2, gla_g, lb_logits, hgrn_g, w_out_b, w_mlp1_b, w_mlp2_b,
      final_g)


def _split_w_in(w_in):
    lr_lo = 2 * GLA_QK + GLA_V
    lr_hi = lr_lo + GLA_GATE_RANK
    pad = jnp.zeros((w_in.shape[0], RANK_PAD - GLA_GATE_RANK), w_in.dtype)
    return (w_in[:, :lr_lo].astype(BF16), w_in[:, lr_hi:].astype(BF16),
            jnp.concatenate([w_in[:, lr_lo:lr_hi], pad], axis=1).astype(BF16))


def kernel(x, c, w_ada, b_ada, w_in, gla_gate_w2, gla_gate_b2, gla_norm_g, hgrn_lb_logits,
           hgrn_norm_g, w_out, w_mlp1, w_mlp2, final_norm_g):
    assert w_ada.shape[0] == 1, "single-layer trunk"
    batch = x.shape[0]
    mod = _modulation(c, w_ada[0], b_ada[0]).reshape(batch, N_MOD, D_MODEL)

    w2_p = jnp.concatenate(
        [gla_gate_w2[0], jnp.zeros((RANK_PAD - GLA_GATE_RANK, GLA_QK), F32)], axis=0).astype(BF16)
    return _layer(x, mod, _split_w_in(w_in[0]), w2_p, gla_gate_b2[0].reshape(1, GLA_QK),
                  gla_norm_g[0].reshape(1, GLA_DV), hgrn_lb_logits,
                  hgrn_norm_g[0].reshape(1, HGRN_DV), w_out[0].astype(BF16),
                  w_mlp1[0].astype(BF16), w_mlp2[0].astype(BF16),
                  final_norm_g.reshape(1, D_MODEL))
```

```python
import functools

import jax
import jax.numpy as jnp
from jax import lax
from jax.experimental import pallas as pl
from jax.experimental.pallas import tpu as pltpu

F32 = jnp.float32
BF16 = jnp.bfloat16

D_MODEL = 1024
GLA_HEADS = 4
GLA_DK = 64
GLA_DV = 128
GLA_QK = GLA_HEADS * GLA_DK
GLA_V = GLA_HEADS * GLA_DV
GLA_GATE_RANK = 16
GLA_GATE_NORMALIZER = 16.0
HGRN_HEADS = 4
HGRN_DK = 128
HGRN_DV = 128
HGRN_K = HGRN_HEADS * HGRN_DK
HGRN_V = HGRN_HEADS * HGRN_DV
CHUNK = 64
D_FF = 4 * D_MODEL
N_MOD = 6
EPS = 1e-6

LANES = 128
SUBLANES = 8
RANK_PAD = LANES

OFF_GQ = 0
OFF_GK = OFF_GQ + GLA_QK
OFF_GV = OFF_GK + GLA_QK
OFF_GG = OFF_GV + GLA_V
OFF_HQ = OFF_GG + GLA_V
OFF_HF = OFF_HQ + HGRN_K
OFF_HI = OFF_HF + HGRN_K
OFF_HG = OFF_HI + HGRN_V
OFF_LR = OFF_HG + HGRN_V
IN_WIDTH_P = OFF_LR + RANK_PAD

TILE = 512
MIX_ROWS = 256
FF_BLOCK = 256
GLA_FILL = (2, 1, 2)
HGRN_FILL = (2, 1, 2)
FILL_UNITS_PER_MIX = 30
VMEM_LIMIT = 56 * 1024 * 1024


def _rms(x):
    return x * lax.rsqrt(jnp.mean(x * x, axis=-1, keepdims=True) + EPS)


def _sigmoid(x):
    return 1.0 / (1.0 + jnp.exp(-x))


def _silu(x):
    return x * _sigmoid(x)


def _log_sigmoid(x):
    return jnp.minimum(x, 0.0) - jnp.log(1.0 + jnp.exp(-jnp.abs(x)))


def _mod_kernel(c_ref, w_ref, b_ref, o_ref):
    cond = _silu(c_ref[...])
    o_ref[...] = jnp.dot(cond.astype(BF16), w_ref[...].astype(BF16),
                         preferred_element_type=F32) + b_ref[...]


def _modulation(c, w_ada, b_ada):
    batch = c.shape[0]
    n_out = w_ada.shape[1]
    blk = D_MODEL
    return pl.pallas_call(
        _mod_kernel,
        grid=(n_out // blk,),
        in_specs=[
            pl.BlockSpec((batch, D_MODEL), lambda j: (0, 0)),
            pl.BlockSpec((D_MODEL, blk), lambda j: (0, j)),
            pl.BlockSpec((1, blk), lambda j: (0, j)),
        ],
        out_specs=pl.BlockSpec((batch, blk), lambda j: (0, j)),
        out_shape=jax.ShapeDtypeStruct((batch, n_out), F32),
        name="adaln_mod",
    )(c, w_ada, b_ada.reshape(1, n_out))


def _chunk_scan(x):
    row = lax.broadcasted_iota(jnp.int32, (SUBLANES, x.shape[1]), 0)
    out = []
    carry = None
    for g in range(x.shape[0] // SUBLANES):
        p = x[g * SUBLANES:(g + 1) * SUBLANES, :]
        s = 1
        while s < SUBLANES:
            p = p + jnp.where(row >= s, pltpu.roll(p, s, 0), 0.0)
            s *= 2
        if carry is not None:
            p = p + carry
        carry = p[SUBLANES - 1:SUBLANES, :]
        out.append(p)
    return jnp.concatenate(out, axis=0)


def _blocks_to_array(blocks):
    return jnp.concatenate([jnp.concatenate(row, axis=1) for row in blocks], axis=0)


def _block_diag(blocks, n):
    zero = jnp.zeros_like(blocks[0])
    return jnp.concatenate(
        [jnp.concatenate([blocks[h] if g == h else zero for g in range(n)], axis=1)
         for h in range(n)], axis=0)


def _gated_linear_attention(q_fn, k_la_fn, v, s_ref, n_heads, dk, dv, fill, fill_units):
    tile = v.shape[0]
    n_chunks = tile // CHUNK
    width = n_heads * dk
    n_lane_blocks = width // LANES
    heads_per_lane_block = LANES // dk if dk < LANES else 1
    head_of_lane = lax.broadcasted_iota(jnp.int32, (CHUNK, width), 1) // dk
    key_le_query = (lax.broadcasted_iota(jnp.int32, (CHUNK, n_heads * CHUNK), 1) % CHUNK
                    <= lax.broadcasted_iota(jnp.int32, (CHUNK, n_heads * CHUNK), 0))

    fill_pre, fill_per_lane_block, fill_per_chunk = fill_units
    fill(fill_pre)
    q_rel = [[None] * n_lane_blocks for _ in range(n_chunks)]
    k_rel = [[None] * n_lane_blocks for _ in range(n_chunks)]
    k_end = [[None] * n_lane_blocks for _ in range(n_chunks)]
    q_in = [[None] * n_lane_blocks for _ in range(n_chunks)]
    last = [[None] * n_lane_blocks for _ in range(n_chunks)]
    for c in range(n_lane_blocks):
        cols = slice(c * LANES, (c + 1) * LANES)
        for j in range(n_chunks):
            rows = slice(j * CHUNK, (j + 1) * CHUNK)
            q = q_fn(rows, cols)
            k, log_a = k_la_fn(rows, cols)
            cum = _chunk_scan(log_a)
            cum_mid = cum[CHUNK // 2 - 1:CHUNK // 2, :]
            cum_last = cum[CHUNK - 1:CHUNK, :]
            q_rel[j][c] = (q * jnp.exp(cum - cum_mid)).astype(BF16)
            k_rel[j][c] = k * jnp.exp(cum_mid - cum)
            k_end[j][c] = k * jnp.exp(cum_last - cum)
            q_in[j][c] = (q * jnp.exp(cum)).astype(BF16)
            last[j][c] = cum_last
        fill(fill_per_lane_block)

    rows8 = lax.broadcasted_iota(jnp.int32, (8, width), 0)
    last8 = jnp.zeros((8, width), F32)
    for j in range(n_chunks):
        last8 = jnp.where(rows8 == j, jnp.concatenate(last[j], axis=1), last8)
    last_pad = jnp.concatenate([last8, jnp.zeros((LANES - 8, width), F32)], axis=0)
    decay_t = jnp.exp(last_pad.T)

    v_b = v.astype(BF16)

    def chunk_scores(j):
        k_j = jnp.concatenate(k_rel[j], axis=1)
        if dk % LANES == 0:
            per_head = [k_j[:, h * dk:(h + 1) * dk] for h in range(n_heads)]
            k_bd = _block_diag(per_head, n_heads)
        else:
            k_bd = jnp.concatenate(
                [jnp.where(head_of_lane == h, k_j, 0.0) for h in range(n_heads)], axis=0)
        return jnp.dot(jnp.concatenate(q_rel[j], axis=1), k_bd.T.astype(BF16),
                       preferred_element_type=F32)

    state = [s_ref[h] for h in range(n_heads)]
    out = []
    scores_next = chunk_scores(0)
    for j in range(n_chunks):
        rows = slice(j * CHUNK, (j + 1) * CHUNK)
        scores = scores_next
        if j + 1 < n_chunks:
            scores_next = chunk_scores(j + 1)
        fill(fill_per_chunk)
        v_heads = [v_b[rows, h * dv:(h + 1) * dv] for h in range(n_heads)]
        v_bd = _block_diag(v_heads, n_heads)
        p = jnp.where(key_le_query, scores, 0.0).astype(BF16)
        o_j = jnp.dot(p, v_bd, preferred_element_type=F32)
        w_state = _block_diag([s.astype(BF16) for s in state], n_heads)
        o_j = o_j + jnp.dot(jnp.concatenate(q_in[j], axis=1), w_state, preferred_element_type=F32)
        out.append(o_j)
        if dk % LANES == 0:
            k_t = jnp.concatenate(k_end[j], axis=0).T.astype(BF16)
            upd = jnp.dot(k_t, v_bd, preferred_element_type=F32)
            upd_heads = [upd[:, h * dv:(h + 1) * dv] for h in range(n_heads)]
        else:
            k_t = jnp.concatenate(k_end[j], axis=0).T.astype(BF16)
            zero = jnp.zeros_like(v_heads[0])
            v_sel = jnp.concatenate(
                [jnp.concatenate([v_heads[h] if h // heads_per_lane_block == c else zero
                                  for h in range(n_heads)], axis=1)
                 for c in range(n_lane_blocks)], axis=0)
            upd = jnp.dot(k_t, v_sel, preferred_element_type=F32)
            upd_heads = [upd[(h % heads_per_lane_block) * dk:(h % heads_per_lane_block + 1) * dk,
                             h * dv:(h + 1) * dv] for h in range(n_heads)]
        state = [decay_t[h * dk:(h + 1) * dk, j:j + 1] * state[h] + upd_heads[h]
                 for h in range(n_heads)]
    for h in range(n_heads):
        s_ref[h] = state[h]
    out = jnp.concatenate(out, axis=0)
    return [out[:, h * dv:(h + 1) * dv] for h in range(n_heads)]


def _head_norm_gate(heads, gain, gate, dv):
    outs = [_rms(o_h) * gain * _silu(gate[:, h * dv:(h + 1) * dv]) for h, o_h in enumerate(heads)]
    return jnp.concatenate(outs, axis=1)


def _token_mixer(x, mod_ref, w_in_refs, w2_ref, b2_ref, gla_g_ref, lb_ref, hgrn_g_ref, w_out_ref,
                 s_gla_ref, s_hgrn_ref, fill, finish):
    shift = mod_ref[0, 0:1, :]
    scale = mod_ref[0, 1:2, :]
    gate = mod_ref[0, 2:3, :]
    fill(2)
    h_b = (_rms(x) * (1.0 + scale) + shift).astype(BF16)

    def proj(off, width):
        for ref in w_in_refs:
            if off < ref.shape[1]:
                return jnp.dot(h_b, ref[:, off:off + width], preferred_element_type=F32)
            off -= ref.shape[1]

    gate_logits = jnp.dot(proj(OFF_LR, RANK_PAD).astype(BF16), w2_ref[...],
                          preferred_element_type=F32) + b2_ref[...]
    gq = proj(OFF_GQ, GLA_QK)
    gk = proj(OFF_GK, GLA_QK)
    o_gla = _gated_linear_attention(
        lambda rows, cols: gq[rows, cols] * (GLA_DK ** -0.5),
        lambda rows, cols: (gk[rows, cols],
                            _log_sigmoid(gate_logits[rows, cols]) * (1.0 / GLA_GATE_NORMALIZER)),
        proj(OFF_GV, GLA_V), s_gla_ref, GLA_HEADS, GLA_DK, GLA_DV, fill, GLA_FILL)
    o_gla = _head_norm_gate(o_gla, gla_g_ref[...], proj(OFF_GG, GLA_V), GLA_DV)
    fill(2)

    lb_e = jnp.exp(lb_ref[...] - jnp.max(lb_ref[...], axis=0, keepdims=True))
    lb = lb_e[0:1, :] / jnp.sum(lb_e, axis=0, keepdims=True)
    hq = proj(OFF_HQ, HGRN_K)
    hf = proj(OFF_HF, HGRN_K)

    def hgrn_k_la(rows, cols):
        forget = lb[:, cols] + (1.0 - lb[:, cols]) * _sigmoid(hf[rows, cols])
        return 1.0 - forget, jnp.log(forget)

    o_hgrn = _gated_linear_attention(
        lambda rows, cols: _silu(hq[rows, cols]), hgrn_k_la, proj(OFF_HI, HGRN_V),
        s_hgrn_ref, HGRN_HEADS, HGRN_DK, HGRN_DV, fill, HGRN_FILL)
    o_hgrn = _head_norm_gate(o_hgrn, hgrn_g_ref[...], proj(OFF_HG, HGRN_V), HGRN_DV)
    finish()

    mixed = jnp.concatenate([o_gla, o_hgrn], axis=1).astype(BF16)
    return x + gate * jnp.dot(mixed, w_out_ref[...], preferred_element_type=F32)


class _ChannelMixer:
    def __init__(self, x, mod_ref, w1_ref, w2_ref, g_ref):
        self.x, self.w1_ref, self.w2_ref, self.g_ref = x, w1_ref, w2_ref, g_ref
        shift = mod_ref[0, 3:4, :]
        scale = mod_ref[0, 4:5, :]
        self.gate = mod_ref[0, 5:6, :]
        self.h_b = (_rms(x) * (1.0 + scale) + shift).astype(BF16)
        self.acc = None
        self.hidden = []
        self.credit = 0.0
        self.up = 0
        self.down = 0

    def _up(self):
        lo = self.up * FF_BLOCK
        a = jnp.maximum(jnp.dot(self.h_b, self.w1_ref[:, lo:lo + FF_BLOCK],
                                preferred_element_type=F32), 0.0)
        self.hidden.append((a * a).astype(BF16))
        self.up += 1

    def _down(self):
        lo = self.down * FF_BLOCK
        y = jnp.dot(self.hidden.pop(0), self.w2_ref[lo:lo + FF_BLOCK, :],
                    preferred_element_type=F32)
        self.acc = y if self.acc is None else self.acc + y
        self.down += 1

    def step(self):
        n_blocks = D_FF // FF_BLOCK
        if self.up < n_blocks and self.up <= self.down + 1:
            self._up()
        elif self.down < n_blocks:
            self._down()

    def fill(self, n):
        self.credit += n * (2 * D_FF // FF_BLOCK) / (FILL_UNITS_PER_MIX * (TILE // MIX_ROWS))
        while self.credit >= 1.0:
            self.step()
            self.credit -= 1.0

    def finish(self):
        while self.down * FF_BLOCK < D_FF:
            self.step()
        return _rms(self.x + self.gate * self.acc) * self.g_ref[...]


def _layer_kernel(tiles_per_seq, x_ref, mod_a_ref, mod_b_ref, w_in_a_ref, w_in_b_ref, w_in_lr_ref,
                  w2_ref, b2_ref, gla_g_ref, lb_ref, hgrn_g_ref, w_out_ref, w_mlp1_ref, w_mlp2_ref,
                  final_g_ref, o_ref,
                  s_gla_ref, s_hgrn_ref, x1_ref):
    i = pl.program_id(0)

    @pl.when(i == 0)
    def _():
        x1_ref[...] = jnp.zeros_like(x1_ref)

    @pl.when(i % tiles_per_seq == 0)
    def _():
        s_gla_ref[...] = jnp.zeros_like(s_gla_ref)
        s_hgrn_ref[...] = jnp.zeros_like(s_hgrn_ref)

    mlp = _ChannelMixer(x1_ref[(i + 1) % 2], mod_b_ref, w_mlp1_ref, w_mlp2_ref, final_g_ref)

    def finish():
        o_ref[0] = mlp.finish()

    n_sub = TILE // MIX_ROWS
    for sub in range(n_sub):
        rows = pl.ds(sub * MIX_ROWS, MIX_ROWS)
        x1_ref[i % 2, rows, :] = _token_mixer(
            x_ref[0, rows, :], mod_a_ref, (w_in_a_ref, w_in_b_ref, w_in_lr_ref), w2_ref, b2_ref,
            gla_g_ref, lb_ref, hgrn_g_ref,
            w_out_ref, s_gla_ref, s_hgrn_ref, mlp.fill, finish if sub == n_sub - 1 else lambda: None)


def _const_spec(shape):
    zeros = (0,) * len(shape)
    return pl.BlockSpec(shape, lambda *_: zeros, pipeline_mode=pl.Buffered(1))


def _layer(x, mod, w_in_parts, w2_p, b2, gla_g, lb_logits, hgrn_g, w_out_b, w_mlp1_b, w_mlp2_b,
           final_g):
    batch, seq, _ = x.shape
    tps = seq // TILE
    n_tiles = batch * tps

    def tile_in(i):
        j = jnp.minimum(i, n_tiles - 1)
        return (j // tps, j % tps, 0)

    def tile_out(i):
        j = jnp.maximum(i - 1, 0)
        return (j // tps, j % tps, 0)

    return pl.pallas_call(
        functools.partial(_layer_kernel, tps),
        grid=(n_tiles + 1,),
        in_specs=[
            pl.BlockSpec((1, TILE, D_MODEL), tile_in),
            pl.BlockSpec((1, N_MOD, D_MODEL), lambda i: (tile_in(i)[0], 0, 0)),
            pl.BlockSpec((1, N_MOD, D_MODEL), lambda i: (tile_out(i)[0], 0, 0)),
            *[_const_spec(w.shape) for w in w_in_parts],
            _const_spec(w2_p.shape),
            _const_spec(b2.shape),
            _const_spec(gla_g.shape),
            _const_spec(lb_logits.shape),
            _const_spec(hgrn_g.shape),
            _const_spec(w_out_b.shape),
            _const_spec(w_mlp1_b.shape),
            _const_spec(w_mlp2_b.shape),
            _const_spec(final_g.shape),
        ],
        out_specs=pl.BlockSpec((1, TILE, D_MODEL), tile_out),
        out_shape=jax.ShapeDtypeStruct(x.shape, F32),
        scratch_shapes=[
            pltpu.VMEM((GLA_HEADS, GLA_DK, GLA_DV), F32),
            pltpu.VMEM((HGRN_HEADS, HGRN_DK, HGRN_DV), F32),
            pltpu.VMEM((2, TILE, D_MODEL), F32),
        ],
        compiler_params=pltpu.CompilerParams(
            dimension_semantics=("arbitrary",), vmem_limit_bytes=VMEM_LIMIT),
        name="hybrid_layer",
    )(x, mod, mod, *w_in_parts, w2_p, b2, gla_g, lb_logits, hgrn_g, w_out_b, w_mlp1_b, w_mlp2_b,
      final_g)


def _split_w_in(w_in):
    lr_lo = 2 * GLA_QK + GLA_V
    lr_hi = lr_lo + GLA_GATE_RANK
    pad = jnp.zeros((w_in.shape[0], RANK_PAD - GLA_GATE_RANK), w_in.dtype)
    return (w_in[:, :lr_lo].astype(BF16), w_in[:, lr_hi:].astype(BF16),
            jnp.concatenate([w_in[:, lr_lo:lr_hi], pad], axis=1).astype(BF16))


def kernel(x, c, w_ada, b_ada, w_in, gla_gate_w2, gla_gate_b2, gla_norm_g, hgrn_lb_logits,
           hgrn_norm_g, w_out, w_mlp1, w_mlp2, final_norm_g):
    assert w_ada.shape[0] == 1, "single-layer trunk"
    batch = x.shape[0]
    mod = _modulation(c, w_ada[0], b_ada[0]).reshape(batch, N_MOD, D_MODEL)

    w2_p = jnp.concatenate(
        [gla_gate_w2[0], jnp.zeros((RANK_PAD - GLA_GATE_RANK, GLA_QK), F32)], axis=0).astype(BF16)
    return _layer(x, mod, _split_w_in(w_in[0]), w2_p, gla_gate_b2[0].reshape(1, GLA_QK),
                  gla_norm_g[0].reshape(1, GLA_DV), hgrn_lb_logits,
                  hgrn_norm_g[0].reshape(1, HGRN_DV), w_out[0].astype(BF16),
                  w_mlp1[0].astype(BF16), w_mlp2[0].astype(BF16),
                  final_norm_g.reshape(1, D_MODEL))
```

```python
import functools

import jax
import jax.numpy as jnp
from jax import lax
from jax.experimental import pallas as pl
from jax.experimental.pallas import tpu as pltpu

F32 = jnp.float32
BF16 = jnp.bfloat16

D_MODEL = 1024
GLA_HEADS = 4
GLA_DK = 64
GLA_DV = 128
GLA_QK = GLA_HEADS * GLA_DK
GLA_V = GLA_HEADS * GLA_DV
GLA_GATE_RANK = 16
GLA_GATE_NORMALIZER = 16.0
HGRN_HEADS = 4
HGRN_DK = 128
HGRN_DV = 128
HGRN_K = HGRN_HEADS * HGRN_DK
HGRN_V = HGRN_HEADS * HGRN_DV
CHUNK = 64
D_FF = 4 * D_MODEL
N_MOD = 6
EPS = 1e-6

LANES = 128
SUBLANES = 8
RANK_PAD = LANES

OFF_GQ = 0
OFF_GK = OFF_GQ + GLA_QK
OFF_GV = OFF_GK + GLA_QK
OFF_GG = OFF_GV + GLA_V
OFF_HQ = OFF_GG + GLA_V
OFF_HF = OFF_HQ + HGRN_K
OFF_HI = OFF_HF + HGRN_K
OFF_HG = OFF_HI + HGRN_V
OFF_LR = OFF_HG + HGRN_V
IN_WIDTH_P = OFF_LR + RANK_PAD

TILE = 512
MIX_ROWS = 256
FF_BLOCK = 256
GLA_FILL = (2, 1, 2)
HGRN_FILL = (2, 1, 2)
FILL_UNITS_PER_MIX = 30
HEADS_PER_DOT = 2
VMEM_LIMIT = 56 * 1024 * 1024


def _rms(x):
    return x * lax.rsqrt(jnp.mean(x * x, axis=-1, keepdims=True) + EPS)


def _sigmoid(x):
    return 1.0 / (1.0 + jnp.exp(-x))


def _silu(x):
    return x * _sigmoid(x)


def _log_sigmoid(x):
    return jnp.minimum(x, 0.0) - jnp.log(1.0 + jnp.exp(-jnp.abs(x)))


def _mod_kernel(c_ref, w_ref, b_ref, o_ref):
    cond = _silu(c_ref[...])
    o_ref[...] = jnp.dot(cond.astype(BF16), w_ref[...].astype(BF16),
                         preferred_element_type=F32) + b_ref[...]


def _modulation(c, w_ada, b_ada):
    batch = c.shape[0]
    n_out = w_ada.shape[1]
    blk = D_MODEL
    return pl.pallas_call(
        _mod_kernel,
        grid=(n_out // blk,),
        in_specs=[
            pl.BlockSpec((batch, D_MODEL), lambda j: (0, 0)),
            pl.BlockSpec((D_MODEL, blk), lambda j: (0, j)),
            pl.BlockSpec((1, blk), lambda j: (0, j)),
        ],
        out_specs=pl.BlockSpec((batch, blk), lambda j: (0, j)),
        out_shape=jax.ShapeDtypeStruct((batch, n_out), F32),
        name="adaln_mod",
    )(c, w_ada, b_ada.reshape(1, n_out))


def _chunk_scan(x):
    row = lax.broadcasted_iota(jnp.int32, (SUBLANES, x.shape[1]), 0)
    out = []
    carry = None
    for g in range(x.shape[0] // SUBLANES):
        p = x[g * SUBLANES:(g + 1) * SUBLANES, :]
        s = 1
        while s < SUBLANES:
            p = p + jnp.where(row >= s, pltpu.roll(p, s, 0), 0.0)
            s *= 2
        if carry is not None:
            p = p + carry
        carry = p[SUBLANES - 1:SUBLANES, :]
        out.append(p)
    return jnp.concatenate(out, axis=0)


def _blocks_to_array(blocks):
    return jnp.concatenate([jnp.concatenate(row, axis=1) for row in blocks], axis=0)


def _block_diag(blocks, n):
    zero = jnp.zeros_like(blocks[0])
    return jnp.concatenate(
        [jnp.concatenate([blocks[h] if g == h else zero for g in range(n)], axis=1)
         for h in range(n)], axis=0)


def _gated_linear_attention(q_fn, k_la_fn, v, s_ref, n_heads, dk, dv, fill, fill_units):
    tile = v.shape[0]
    n_chunks = tile // CHUNK
    width = n_heads * dk
    n_lane_blocks = width // LANES

    fill_pre, fill_per_lane_block, fill_per_chunk = fill_units
    fill(fill_pre)
    q_rel = [[None] * n_lane_blocks for _ in range(n_chunks)]
    k_rel = [[None] * n_lane_blocks for _ in range(n_chunks)]
    k_end = [[None] * n_lane_blocks for _ in range(n_chunks)]
    q_in = [[None] * n_lane_blocks for _ in range(n_chunks)]
    last = [[None] * n_lane_blocks for _ in range(n_chunks)]
    for c in range(n_lane_blocks):
        cols = slice(c * LANES, (c + 1) * LANES)
        for j in range(n_chunks):
            rows = slice(j * CHUNK, (j + 1) * CHUNK)
            q = q_fn(rows, cols)
            k, log_a = k_la_fn(rows, cols)
            cum = _chunk_scan(log_a)
            cum_mid = cum[CHUNK // 2 - 1:CHUNK // 2, :]
            cum_last = cum[CHUNK - 1:CHUNK, :]
            q_rel[j][c] = (q * jnp.exp(cum - cum_mid)).astype(BF16)
            k_rel[j][c] = k * jnp.exp(cum_mid - cum)
            k_end[j][c] = k * jnp.exp(cum_last - cum)
            q_in[j][c] = (q * jnp.exp(cum)).astype(BF16)
            last[j][c] = cum_last
        fill(fill_per_lane_block)

    rows8 = lax.broadcasted_iota(jnp.int32, (8, width), 0)
    last8 = jnp.zeros((8, width), F32)
    for j in range(n_chunks):
        last8 = jnp.where(rows8 == j, jnp.concatenate(last[j], axis=1), last8)
    last_pad = jnp.concatenate([last8, jnp.zeros((LANES - 8, width), F32)], axis=0)
    decay_t = jnp.exp(last_pad.T)

    v_b = v.astype(BF16)
    hpd = HEADS_PER_DOT
    n_groups = n_heads // hpd
    gw = hpd * dk
    key_le_query = (lax.broadcasted_iota(jnp.int32, (CHUNK, hpd * CHUNK), 1) % CHUNK
                    <= lax.broadcasted_iota(jnp.int32, (CHUNK, hpd * CHUNK), 0))
    head_of_lane = lax.broadcasted_iota(jnp.int32, (CHUNK, gw), 1) // dk

    def group_scores(j, g):
        k_g = jnp.concatenate(k_rel[j], axis=1)[:, g * gw:(g + 1) * gw]
        if dk % LANES == 0:
            k_bd = _block_diag([k_g[:, b * dk:(b + 1) * dk] for b in range(hpd)], hpd)
        else:
            k_bd = jnp.concatenate(
                [jnp.where(head_of_lane == b, k_g, 0.0) for b in range(hpd)], axis=0)
        q_g = jnp.concatenate(q_rel[j], axis=1)[:, g * gw:(g + 1) * gw]
        return jnp.dot(q_g, k_bd.T.astype(BF16), preferred_element_type=F32)

    state = [s_ref[h] for h in range(n_heads)]
    out = []
    scores_next = [group_scores(0, g) for g in range(n_groups)]
    for j in range(n_chunks):
        rows = slice(j * CHUNK, (j + 1) * CHUNK)
        scores = scores_next
        if j + 1 < n_chunks:
            scores_next = [group_scores(j + 1, g) for g in range(n_groups)]
        fill(fill_per_chunk)
        q_in_j = jnp.concatenate(q_in[j], axis=1)
        out_j = []
        new_state = list(state)
        for g in range(n_groups):
            heads = range(g * hpd, (g + 1) * hpd)
            v_heads = [v_b[rows, h * dv:(h + 1) * dv] for h in heads]
            v_bd = _block_diag(v_heads, hpd)
            p = jnp.where(key_le_query, scores[g], 0.0).astype(BF16)
            o_g = jnp.dot(p, v_bd, preferred_element_type=F32)
            w_state = _block_diag([state[h].astype(BF16) for h in heads], hpd)
            o_g = o_g + jnp.dot(q_in_j[:, g * gw:(g + 1) * gw], w_state,
                                preferred_element_type=F32)
            out_j.append(o_g)
            if dk % LANES == 0:
                k_t = jnp.concatenate([k_end[j][h] for h in heads], axis=0).T.astype(BF16)
                upd = jnp.dot(k_t, v_bd, preferred_element_type=F32)
                upd_heads = [upd[:, b * dv:(b + 1) * dv] for b in range(hpd)]
            else:
                k_t = k_end[j][g].T.astype(BF16)
                upd = jnp.dot(k_t, jnp.concatenate(v_heads, axis=1),
                              preferred_element_type=F32)
                upd_heads = [upd[b * dk:(b + 1) * dk, b * dv:(b + 1) * dv] for b in range(hpd)]
            for b, h in enumerate(heads):
                new_state[h] = decay_t[h * dk:(h + 1) * dk, j:j + 1] * state[h] + upd_heads[b]
        state = new_state
        out.append(jnp.concatenate(out_j, axis=1))
    for h in range(n_heads):
        s_ref[h] = state[h]
    out = jnp.concatenate(out, axis=0)
    return [out[:, h * dv:(h + 1) * dv] for h in range(n_heads)]


def _head_norm_gate(heads, gain, gate, dv):
    outs = [_rms(o_h) * gain * _silu(gate[:, h * dv:(h + 1) * dv]) for h, o_h in enumerate(heads)]
    return jnp.concatenate(outs, axis=1)


def _token_mixer(x, mod_ref, w_in_refs, w2_ref, b2_ref, gla_g_ref, lb_ref, hgrn_g_ref, w_out_ref,
                 s_gla_ref, s_hgrn_ref, fill, finish):
    shift = mod_ref[0, 0:1, :]
    scale = mod_ref[0, 1:2, :]
    gate = mod_ref[0, 2:3, :]
    fill(2)
    h_b = (_rms(x) * (1.0 + scale) + shift).astype(BF16)

    def proj(off, width):
        for ref in w_in_refs:
            if off < ref.shape[1]:
                return jnp.dot(h_b, ref[:, off:off + width], preferred_element_type=F32)
            off -= ref.shape[1]

    gate_logits = jnp.dot(proj(OFF_LR, RANK_PAD).astype(BF16), w2_ref[...],
                          preferred_element_type=F32) + b2_ref[...]
    gq = proj(OFF_GQ, GLA_QK)
    gk = proj(OFF_GK, GLA_QK)
    o_gla = _gated_linear_attention(
        lambda rows, cols: gq[rows, cols] * (GLA_DK ** -0.5),
        lambda rows, cols: (gk[rows, cols],
                            _log_sigmoid(gate_logits[rows, cols]) * (1.0 / GLA_GATE_NORMALIZER)),
        proj(OFF_GV, GLA_V), s_gla_ref, GLA_HEADS, GLA_DK, GLA_DV, fill, GLA_FILL)
    o_gla = _head_norm_gate(o_gla, gla_g_ref[...], proj(OFF_GG, GLA_V), GLA_DV)
    fill(2)

    lb_e = jnp.exp(lb_ref[...] - jnp.max(lb_ref[...], axis=0, keepdims=True))
    lb = lb_e[0:1, :] / jnp.sum(lb_e, axis=0, keepdims=True)
    hq = proj(OFF_HQ, HGRN_K)
    hf = proj(OFF_HF, HGRN_K)

    def hgrn_k_la(rows, cols):
        forget = lb[:, cols] + (1.0 - lb[:, cols]) * _sigmoid(hf[rows, cols])
        return 1.0 - forget, jnp.log(forget)

    o_hgrn = _gated_linear_attention(
        lambda rows, cols: _silu(hq[rows, cols]), hgrn_k_la, proj(OFF_HI, HGRN_V),
        s_hgrn_ref, HGRN_HEADS, HGRN_DK, HGRN_DV, fill, HGRN_FILL)
    o_hgrn = _head_norm_gate(o_hgrn, hgrn_g_ref[...], proj(OFF_HG, HGRN_V), HGRN_DV)
    finish()

    mixed = jnp.concatenate([o_gla, o_hgrn], axis=1).astype(BF16)
    return x + gate * jnp.dot(mixed, w_out_ref[...], preferred_element_type=F32)


class _ChannelMixer:
    def __init__(self, x, mod_ref, w1_ref, w2_ref, g_ref):
        self.x, self.w1_ref, self.w2_ref, self.g_ref = x, w1_ref, w2_ref, g_ref
        shift = mod_ref[0, 3:4, :]
        scale = mod_ref[0, 4:5, :]
        self.gate = mod_ref[0, 5:6, :]
        self.h_b = (_rms(x) * (1.0 + scale) + shift).astype(BF16)
        self.acc = None
        self.hidden = []
        self.credit = 0.0
        self.up = 0
        self.down = 0

    def _up(self):
        lo = self.up * FF_BLOCK
        a = jnp.maximum(jnp.dot(self.h_b, self.w1_ref[:, lo:lo + FF_BLOCK],
                                preferred_element_type=F32), 0.0)
        self.hidden.append((a * a).astype(BF16))
        self.up += 1

    def _down(self):
        lo = self.down * FF_BLOCK
        y = jnp.dot(self.hidden.pop(0), self.w2_ref[lo:lo + FF_BLOCK, :],
                    preferred_element_type=F32)
        self.acc = y if self.acc is None else self.acc + y
        self.down += 1

    def step(self):
        n_blocks = D_FF // FF_BLOCK
        if self.up < n_blocks and self.up <= self.down + 1:
            self._up()
        elif self.down < n_blocks:
            self._down()

    def fill(self, n):
        self.credit += n * (2 * D_FF // FF_BLOCK) / (FILL_UNITS_PER_MIX * (TILE // MIX_ROWS))
        while self.credit >= 1.0:
            self.step()
            self.credit -= 1.0

    def finish(self):
        while self.down * FF_BLOCK < D_FF:
            self.step()
        return _rms(self.x + self.gate * self.acc) * self.g_ref[...]


def _layer_kernel(tiles_per_seq, x_ref, mod_a_ref, mod_b_ref, w_in_a_ref, w_in_b_ref, w_in_lr_ref,
                  w2_ref, b2_ref, gla_g_ref, lb_ref, hgrn_g_ref, w_out_ref, w_mlp1_ref, w_mlp2_ref,
                  final_g_ref, o_ref,
                  s_gla_ref, s_hgrn_ref, x1_ref):
    i = pl.program_id(0)

    @pl.when(i == 0)
    def _():
        x1_ref[...] = jnp.zeros_like(x1_ref)

    @pl.when(i % tiles_per_seq == 0)
    def _():
        s_gla_ref[...] = jnp.zeros_like(s_gla_ref)
        s_hgrn_ref[...] = jnp.zeros_like(s_hgrn_ref)

    mlp = _ChannelMixer(x1_ref[(i + 1) % 2], mod_b_ref, w_mlp1_ref, w_mlp2_ref, final_g_ref)

    def finish():
        o_ref[0] = mlp.finish()

    n_sub = TILE // MIX_ROWS
    for sub in range(n_sub):
        rows = pl.ds(sub * MIX_ROWS, MIX_ROWS)
        x1_ref[i % 2, rows, :] = _token_mixer(
            x_ref[0, rows, :], mod_a_ref, (w_in_a_ref, w_in_b_ref, w_in_lr_ref), w2_ref, b2_ref,
            gla_g_ref, lb_ref, hgrn_g_ref,
            w_out_ref, s_gla_ref, s_hgrn_ref, mlp.fill, finish if sub == n_sub - 1 else lambda: None)


def _const_spec(shape):
    zeros = (0,) * len(shape)
    return pl.BlockSpec(shape, lambda *_: zeros, pipeline_mode=pl.Buffered(1))


def _layer(x, mod, w_in_parts, w2_p, b2, gla_g, lb_logits, hgrn_g, w_out_b, w_mlp1_b, w_mlp2_b,
           final_g):
    batch, seq, _ = x.shape
    tps = seq // TILE
    n_tiles = batch * tps

    def tile_in(i):
        j = jnp.minimum(i, n_tiles - 1)
        return (j // tps, j % tps, 0)

    def tile_out(i):
        j = jnp.maximum(i - 1, 0)
        return (j // tps, j % tps, 0)

    return pl.pallas_call(
        functools.partial(_layer_kernel, tps),
        grid=(n_tiles + 1,),
        in_specs=[
            pl.BlockSpec((1, TILE, D_MODEL), tile_in),
            pl.BlockSpec((1, N_MOD, D_MODEL), lambda i: (tile_in(i)[0], 0, 0)),
            pl.BlockSpec((1, N_MOD, D_MODEL), lambda i: (tile_out(i)[0], 0, 0)),
            *[_const_spec(w.shape) for w in w_in_parts],
            _const_spec(w2_p.shape),
            _const_spec(b2.shape),
            _const_spec(gla_g.shape),
            _const_spec(lb_logits.shape),
            _const_spec(hgrn_g.shape),
            _const_spec(w_out_b.shape),
            _const_spec(w_mlp1_b.shape),
            _const_spec(w_mlp2_b.shape),
            _const_spec(final_g.shape),
        ],
        out_specs=pl.BlockSpec((1, TILE, D_MODEL), tile_out),
        out_shape=jax.ShapeDtypeStruct(x.shape, F32),
        scratch_shapes=[
            pltpu.VMEM((GLA_HEADS, GLA_DK, GLA_DV), F32),
            pltpu.VMEM((HGRN_HEADS, HGRN_DK, HGRN_DV), F32),
            pltpu.VMEM((2, TILE, D_MODEL), F32),
        ],
        compiler_params=pltpu.CompilerParams(
            dimension_semantics=("arbitrary",), vmem_limit_bytes=VMEM_LIMIT),
        name="hybrid_layer",
    )(x, mod, mod, *w_in_parts, w2_p, b2, gla_g, lb_logits, hgrn_g, w_out_b, w_mlp1_b, w_mlp2_b,
      final_g)


def _split_w_in(w_in):
    lr_lo = 2 * GLA_QK + GLA_V
    lr_hi = lr_lo + GLA_GATE_RANK
    pad = jnp.zeros((w_in.shape[0], RANK_PAD - GLA_GATE_RANK), w_in.dtype)
    return (w_in[:, :lr_lo].astype(BF16), w_in[:, lr_hi:].astype(BF16),
            jnp.concatenate([w_in[:, lr_lo:lr_hi], pad], axis=1).astype(BF16))


def kernel(x, c, w_ada, b_ada, w_in, gla_gate_w2, gla_gate_b2, gla_norm_g, hgrn_lb_logits,
           hgrn_norm_g, w_out, w_mlp1, w_mlp2, final_norm_g):
    assert w_ada.shape[0] == 1, "single-layer trunk"
    batch = x.shape[0]
    mod = _modulation(c, w_ada[0], b_ada[0]).reshape(batch, N_MOD, D_MODEL)

    w2_p = jnp.concatenate(
        [gla_gate_w2[0], jnp.zeros((RANK_PAD - GLA_GATE_RANK, GLA_QK), F32)], axis=0).astype(BF16)
    return _layer(x, mod, _split_w_in(w_in[0]), w2_p, gla_gate_b2[0].reshape(1, GLA_QK),
                  gla_norm_g[0].reshape(1, GLA_DV), hgrn_lb_logits,
                  hgrn_norm_g[0].reshape(1, HGRN_DV), w_out[0].astype(BF16),
                  w_mlp1[0].astype(BF16), w_mlp2[0].astype(BF16),
                  final_norm_g.reshape(1, D_MODEL))
```

```python
import functools

import jax
import jax.numpy as jnp
from jax import lax
from jax.experimental import pallas as pl
from jax.experimental.pallas import tpu as pltpu

F32 = jnp.float32
BF16 = jnp.bfloat16

D_MODEL = 1024
GLA_HEADS = 4
GLA_DK = 64
GLA_DV = 128
GLA_QK = GLA_HEADS * GLA_DK
GLA_V = GLA_HEADS * GLA_DV
GLA_GATE_RANK = 16
GLA_GATE_NORMALIZER = 16.0
HGRN_HEADS = 4
HGRN_DK = 128
HGRN_DV = 128
HGRN_K = HGRN_HEADS * HGRN_DK
HGRN_V = HGRN_HEADS * HGRN_DV
CHUNK = 64
D_FF = 4 * D_MODEL
N_MOD = 6
EPS = 1e-6

LANES = 128
SUBLANES = 8
RANK_PAD = LANES

OFF_GQ = 0
OFF_GK = OFF_GQ + GLA_QK
OFF_GV = OFF_GK + GLA_QK
OFF_LR = OFF_GV + GLA_V
OFF_GG = OFF_LR + RANK_PAD
OFF_HQ = OFF_GG + GLA_V
OFF_HF = OFF_HQ + HGRN_K
OFF_HI = OFF_HF + HGRN_K
OFF_HG = OFF_HI + HGRN_V
IN_WIDTH_P = OFF_HG + HGRN_V

TILE = 256
MIX_ROWS = 256
FF_BLOCK = 256
GLA_FILL = (2, 1, 2)
HGRN_FILL = (2, 1, 2)
FILL_UNITS_PER_MIX = 30
HEADS_PER_DOT = 2
VMEM_LIMIT = 56 * 1024 * 1024


def _rms(x):
    return x * lax.rsqrt(jnp.mean(x * x, axis=-1, keepdims=True) + EPS)


def _sigmoid(x):
    return 1.0 / (1.0 + jnp.exp(-x))


def _silu(x):
    return x * _sigmoid(x)


def _log_sigmoid(x):
    return jnp.minimum(x, 0.0) - jnp.log(1.0 + jnp.exp(-jnp.abs(x)))


def _mod_kernel(c_ref, w_ref, b_ref, o_ref):
    cond = _silu(c_ref[...])
    o_ref[...] = jnp.dot(cond.astype(BF16), w_ref[...].astype(BF16),
                         preferred_element_type=F32) + b_ref[...]


def _modulation(c, w_ada, b_ada):
    batch = c.shape[0]
    n_out = w_ada.shape[1]
    blk = D_MODEL
    return pl.pallas_call(
        _mod_kernel,
        grid=(n_out // blk,),
        in_specs=[
            pl.BlockSpec((batch, D_MODEL), lambda j: (0, 0)),
            pl.BlockSpec((D_MODEL, blk), lambda j: (0, j)),
            pl.BlockSpec((1, blk), lambda j: (0, j)),
        ],
        out_specs=pl.BlockSpec((batch, blk), lambda j: (0, j)),
        out_shape=jax.ShapeDtypeStruct((batch, n_out), F32),
        name="adaln_mod",
    )(c, w_ada, b_ada.reshape(1, n_out))


def _chunk_scan(x):
    row = lax.broadcasted_iota(jnp.int32, (SUBLANES, x.shape[1]), 0)
    out = []
    carry = None
    for g in range(x.shape[0] // SUBLANES):
        p = x[g * SUBLANES:(g + 1) * SUBLANES, :]
        s = 1
        while s < SUBLANES:
            p = p + jnp.where(row >= s, pltpu.roll(p, s, 0), 0.0)
            s *= 2
        if carry is not None:
            p = p + carry
        carry = p[SUBLANES - 1:SUBLANES, :]
        out.append(p)
    return jnp.concatenate(out, axis=0)


def _block_diag(blocks, n):
    zero = jnp.zeros_like(blocks[0])
    return jnp.concatenate(
        [jnp.concatenate([blocks[h] if g == h else zero for g in range(n)], axis=1)
         for h in range(n)], axis=0)


def _gated_linear_attention(q_fn, k_la_fn, v, s_ref, n_heads, dk, dv, fill, fill_units):
    tile = v.shape[0]
    n_chunks = tile // CHUNK
    width = n_heads * dk
    n_lane_blocks = width // LANES

    fill_pre, fill_per_lane_block, fill_per_chunk = fill_units
    fill(fill_pre)
    q_rel = [[None] * n_lane_blocks for _ in range(n_chunks)]
    k_rel = [[None] * n_lane_blocks for _ in range(n_chunks)]
    k_end = [[None] * n_lane_blocks for _ in range(n_chunks)]
    q_in = [[None] * n_lane_blocks for _ in range(n_chunks)]
    last = [[None] * n_lane_blocks for _ in range(n_chunks)]
    for c in range(n_lane_blocks):
        cols = slice(c * LANES, (c + 1) * LANES)
        for j in range(n_chunks):
            rows = slice(j * CHUNK, (j + 1) * CHUNK)
            q = q_fn(rows, cols)
            k, log_a = k_la_fn(rows, cols)
            cum = _chunk_scan(log_a)
            cum_mid = cum[CHUNK // 2 - 1:CHUNK // 2, :]
            cum_last = cum[CHUNK - 1:CHUNK, :]
            q_rel[j][c] = (q * jnp.exp(cum - cum_mid)).astype(BF16)
            k_rel[j][c] = k * jnp.exp(cum_mid - cum)
            k_end[j][c] = k * jnp.exp(cum_last - cum)
            q_in[j][c] = (q * jnp.exp(cum)).astype(BF16)
            last[j][c] = cum_last
        fill(fill_per_lane_block)

    rows8 = lax.broadcasted_iota(jnp.int32, (8, width), 0)
    last8 = jnp.zeros((8, width), F32)
    for j in range(n_chunks):
        last8 = jnp.where(rows8 == j, jnp.concatenate(last[j], axis=1), last8)
    last_pad = jnp.concatenate([last8, jnp.zeros((LANES - 8, width), F32)], axis=0)
    decay_t = jnp.exp(last_pad.T)

    v_b = v.astype(BF16)
    hpd = HEADS_PER_DOT
    n_groups = n_heads // hpd
    gw = hpd * dk
    key_le_query = (lax.broadcasted_iota(jnp.int32, (CHUNK, hpd * CHUNK), 1) % CHUNK
                    <= lax.broadcasted_iota(jnp.int32, (CHUNK, hpd * CHUNK), 0))
    head_of_lane = lax.broadcasted_iota(jnp.int32, (CHUNK, gw), 1) // dk

    def group_scores(j, g):
        k_g = jnp.concatenate(k_rel[j], axis=1)[:, g * gw:(g + 1) * gw]
        if dk % LANES == 0:
            k_bd = _block_diag([k_g[:, b * dk:(b + 1) * dk] for b in range(hpd)], hpd)
        else:
            k_bd = jnp.concatenate(
                [jnp.where(head_of_lane == b, k_g, 0.0) for b in range(hpd)], axis=0)
        q_g = jnp.concatenate(q_rel[j], axis=1)[:, g * gw:(g + 1) * gw]
        return jnp.dot(q_g, k_bd.T.astype(BF16), preferred_element_type=F32)

    state = [s_ref[h] for h in range(n_heads)]
    out = []
    scores_next = [group_scores(0, g) for g in range(n_groups)]
    for j in range(n_chunks):
        rows = slice(j * CHUNK, (j + 1) * CHUNK)
        scores = scores_next
        if j + 1 < n_chunks:
            scores_next = [group_scores(j + 1, g) for g in range(n_groups)]
        fill(fill_per_chunk)
        q_in_j = jnp.concatenate(q_in[j], axis=1)
        out_j = []
        new_state = list(state)
        for g in range(n_groups):
            heads = range(g * hpd, (g + 1) * hpd)
            v_heads = [v_b[rows, h * dv:(h + 1) * dv] for h in heads]
            v_bd = _block_diag(v_heads, hpd)
            p = jnp.where(key_le_query, scores[g], 0.0).astype(BF16)
            w_state = _block_diag([state[h].astype(BF16) for h in heads], hpd)
            o_g = jnp.dot(jnp.concatenate([p, q_in_j[:, g * gw:(g + 1) * gw]], axis=1),
                          jnp.concatenate([v_bd, w_state], axis=0), preferred_element_type=F32)
            out_j.append(o_g)
            if dk % LANES == 0:
                k_t = jnp.concatenate([k_end[j][h] for h in heads], axis=0).T.astype(BF16)
                upd = jnp.dot(k_t, v_bd, preferred_element_type=F32)
                upd_heads = [upd[:, b * dv:(b + 1) * dv] for b in range(hpd)]
            else:
                k_t = k_end[j][g].T.astype(BF16)
                upd = jnp.dot(k_t, jnp.concatenate(v_heads, axis=1),
                              preferred_element_type=F32)
                upd_heads = [upd[b * dk:(b + 1) * dk, b * dv:(b + 1) * dv] for b in range(hpd)]
            for b, h in enumerate(heads):
                new_state[h] = decay_t[h * dk:(h + 1) * dk, j:j + 1] * state[h] + upd_heads[b]
        state = new_state
        out.append(jnp.concatenate(out_j, axis=1))
    for h in range(n_heads):
        s_ref[h] = state[h]
    out = jnp.concatenate(out, axis=0)
    return [out[:, h * dv:(h + 1) * dv] for h in range(n_heads)]


def _head_norm_gate(heads, gain, gate, dv):
    outs = [_rms(o_h) * gain * _silu(gate[:, h * dv:(h + 1) * dv]) for h, o_h in enumerate(heads)]
    return jnp.concatenate(outs, axis=1)


def _token_mixer(x, mod_ref, w_in_refs, w2_ref, b2_ref, gla_g_ref, lb_ref, hgrn_g_ref, w_out_ref,
                 s_gla_ref, s_hgrn_ref, fill, finish):
    shift = mod_ref[0, 0:1, :]
    scale = mod_ref[0, 1:2, :]
    gate = mod_ref[0, 2:3, :]
    fill(2)
    h_b = (_rms(x) * (1.0 + scale) + shift).astype(BF16)

    pieces = [jnp.dot(h_b, ref[...], preferred_element_type=F32) for ref in w_in_refs]

    def proj(off, width):
        for piece in pieces:
            if off < piece.shape[1]:
                return piece[:, off:off + width]
            off -= piece.shape[1]

    gate_logits = jnp.dot(proj(OFF_LR, RANK_PAD).astype(BF16), w2_ref[...],
                          preferred_element_type=F32) + b2_ref[...]
    gq = proj(OFF_GQ, GLA_QK)
    gk = proj(OFF_GK, GLA_QK)
    o_gla = _gated_linear_attention(
        lambda rows, cols: gq[rows, cols] * (GLA_DK ** -0.5),
        lambda rows, cols: (gk[rows, cols],
                            _log_sigmoid(gate_logits[rows, cols]) * (1.0 / GLA_GATE_NORMALIZER)),
        proj(OFF_GV, GLA_V), s_gla_ref, GLA_HEADS, GLA_DK, GLA_DV, fill, GLA_FILL)
    o_gla = _head_norm_gate(o_gla, gla_g_ref[...], proj(OFF_GG, GLA_V), GLA_DV)
    fill(2)

    lb_e = jnp.exp(lb_ref[...] - jnp.max(lb_ref[...], axis=0, keepdims=True))
    lb = lb_e[0:1, :] / jnp.sum(lb_e, axis=0, keepdims=True)
    hq = proj(OFF_HQ, HGRN_K)
    hf = proj(OFF_HF, HGRN_K)

    def hgrn_k_la(rows, cols):
        forget = lb[:, cols] + (1.0 - lb[:, cols]) * _sigmoid(hf[rows, cols])
        return 1.0 - forget, jnp.log(forget)

    o_hgrn = _gated_linear_attention(
        lambda rows, cols: _silu(hq[rows, cols]), hgrn_k_la, proj(OFF_HI, HGRN_V),
        s_hgrn_ref, HGRN_HEADS, HGRN_DK, HGRN_DV, fill, HGRN_FILL)
    o_hgrn = _head_norm_gate(o_hgrn, hgrn_g_ref[...], proj(OFF_HG, HGRN_V), HGRN_DV)
    finish()

    mixed = jnp.concatenate([o_gla, o_hgrn], axis=1).astype(BF16)
    return x + gate * jnp.dot(mixed, w_out_ref[...], preferred_element_type=F32)


class _ChannelMixer:
    def __init__(self, x, mod_ref, w1_ref, w2_ref, g_ref):
        self.x, self.w1_ref, self.w2_ref, self.g_ref = x, w1_ref, w2_ref, g_ref
        shift = mod_ref[0, 3:4, :]
        scale = mod_ref[0, 4:5, :]
        self.gate = mod_ref[0, 5:6, :]
        self.h_b = (_rms(x) * (1.0 + scale) + shift).astype(BF16)
        self.acc = None
        self.hidden = []
        self.credit = 0.0
        self.up = 0
        self.down = 0

    def _up(self):
        lo = self.up * FF_BLOCK
        a = jnp.maximum(jnp.dot(self.h_b, self.w1_ref[:, lo:lo + FF_BLOCK],
                                preferred_element_type=F32), 0.0)
        self.hidden.append((a * a).astype(BF16))
        self.up += 1

    def _down(self):
        lo = self.down * FF_BLOCK
        y = jnp.dot(self.hidden.pop(0), self.w2_ref[lo:lo + FF_BLOCK, :],
                    preferred_element_type=F32)
        self.acc = y if self.acc is None else self.acc + y
        self.down += 1

    def step(self):
        n_blocks = D_FF // FF_BLOCK
        if self.up < n_blocks and self.up <= self.down + 1:
            self._up()
        elif self.down < n_blocks:
            self._down()

    def fill(self, n):
        self.credit += n * (2 * D_FF // FF_BLOCK) / (FILL_UNITS_PER_MIX * (TILE // MIX_ROWS))
        while self.credit >= 1.0:
            self.step()
            self.credit -= 1.0

    def finish(self):
        while self.down * FF_BLOCK < D_FF:
            self.step()
        return _rms(self.x + self.gate * self.acc) * self.g_ref[...]


def _layer_kernel(tiles_per_seq, x_ref, mod_a_ref, mod_b_ref, w_in_a_ref, w_in_b_ref,
                  w2_ref, b2_ref, gla_g_ref, lb_ref, hgrn_g_ref, w_out_ref, w_mlp1_ref, w_mlp2_ref,
                  final_g_ref, o_ref,
                  s_gla_ref, s_hgrn_ref, x1_ref):
    i = pl.program_id(0)

    @pl.when(i == 0)
    def _():
        x1_ref[...] = jnp.zeros_like(x1_ref)

    @pl.when(i % tiles_per_seq == 0)
    def _():
        s_gla_ref[...] = jnp.zeros_like(s_gla_ref)
        s_hgrn_ref[...] = jnp.zeros_like(s_hgrn_ref)

    mlp = _ChannelMixer(x1_ref[(i + 1) % 2], mod_b_ref, w_mlp1_ref, w_mlp2_ref, final_g_ref)

    def finish():
        o_ref[0] = mlp.finish()

    n_sub = TILE // MIX_ROWS
    for sub in range(n_sub):
        rows = pl.ds(sub * MIX_ROWS, MIX_ROWS)
        x1_ref[i % 2, rows, :] = _token_mixer(
            x_ref[0, rows, :], mod_a_ref, (w_in_a_ref, w_in_b_ref), w2_ref, b2_ref,
            gla_g_ref, lb_ref, hgrn_g_ref,
            w_out_ref, s_gla_ref, s_hgrn_ref, mlp.fill, finish if sub == n_sub - 1 else lambda: None)


def _const_spec(shape):
    zeros = (0,) * len(shape)
    return pl.BlockSpec(shape, lambda *_: zeros, pipeline_mode=pl.Buffered(1))


def _layer(x, mod, w_in_parts, w2_p, b2, gla_g, lb_logits, hgrn_g, w_out_b, w_mlp1_b, w_mlp2_b,
           final_g):
    batch, seq, _ = x.shape
    tps = seq // TILE
    n_tiles = batch * tps

    def tile_in(i):
        j = jnp.minimum(i, n_tiles - 1)
        return (j // tps, j % tps, 0)

    def tile_out(i):
        j = jnp.maximum(i - 1, 0)
        return (j // tps, j % tps, 0)

    return pl.pallas_call(
        functools.partial(_layer_kernel, tps),
        grid=(n_tiles + 1,),
        in_specs=[
            pl.BlockSpec((1, TILE, D_MODEL), tile_in),
            pl.BlockSpec((1, N_MOD, D_MODEL), lambda i: (tile_in(i)[0], 0, 0)),
            pl.BlockSpec((1, N_MOD, D_MODEL), lambda i: (tile_out(i)[0], 0, 0)),
            *[_const_spec(w.shape) for w in w_in_parts],
            _const_spec(w2_p.shape),
            _const_spec(b2.shape),
            _const_spec(gla_g.shape),
            _const_spec(lb_logits.shape),
            _const_spec(hgrn_g.shape),
            _const_spec(w_out_b.shape),
            _const_spec(w_mlp1_b.shape),
            _const_spec(w_mlp2_b.shape),
            _const_spec(final_g.shape),
        ],
        out_specs=pl.BlockSpec((1, TILE, D_MODEL), tile_out),
        out_shape=jax.ShapeDtypeStruct(x.shape, F32),
        scratch_shapes=[
            pltpu.VMEM((GLA_HEADS, GLA_DK, GLA_DV), F32),
            pltpu.VMEM((HGRN_HEADS, HGRN_DK, HGRN_DV), F32),
            pltpu.VMEM((2, TILE, D_MODEL), F32),
        ],
        compiler_params=pltpu.CompilerParams(
            dimension_semantics=("arbitrary",), vmem_limit_bytes=VMEM_LIMIT),
        name="hybrid_layer",
    )(x, mod, mod, *w_in_parts, w2_p, b2, gla_g, lb_logits, hgrn_g, w_out_b, w_mlp1_b, w_mlp2_b,
      final_g)


def _split_w_in(w_in):
    lr_hi = 2 * GLA_QK + GLA_V + GLA_GATE_RANK
    pad = jnp.zeros((w_in.shape[0], RANK_PAD - GLA_GATE_RANK), w_in.dtype)
    return (jnp.concatenate([w_in[:, :lr_hi], pad], axis=1).astype(BF16),
            w_in[:, lr_hi:].astype(BF16))


def kernel(x, c, w_ada, b_ada, w_in, gla_gate_w2, gla_gate_b2, gla_norm_g, hgrn_lb_logits,
           hgrn_norm_g, w_out, w_mlp1, w_mlp2, final_norm_g):
    assert w_ada.shape[0] == 1, "single-layer trunk"
    batch = x.shape[0]
    mod = _modulation(c, w_ada[0], b_ada[0]).reshape(batch, N_MOD, D_MODEL)

    w2_p = jnp.concatenate(
        [gla_gate_w2[0], jnp.zeros((RANK_PAD - GLA_GATE_RANK, GLA_QK), F32)], axis=0).astype(BF16)
    return _layer(x, mod, _split_w_in(w_in[0]), w2_p, gla_gate_b2[0].reshape(1, GLA_QK),
                  gla_norm_g[0].reshape(1, GLA_DV), hgrn_lb_logits,
                  hgrn_norm_g[0].reshape(1, HGRN_DV), w_out[0].astype(BF16),
                  w_mlp1[0].astype(BF16), w_mlp2[0].astype(BF16),
                  final_norm_g.reshape(1, D_MODEL))
```

```python
import functools

import jax
import jax.numpy as jnp
from jax import lax
from jax.experimental import pallas as pl
from jax.experimental.pallas import tpu as pltpu

F32 = jnp.float32
BF16 = jnp.bfloat16

D_MODEL = 1024
GLA_HEADS = 4
GLA_DK = 64
GLA_DV = 128
GLA_QK = GLA_HEADS * GLA_DK
GLA_V = GLA_HEADS * GLA_DV
GLA_GATE_RANK = 16
GLA_GATE_NORMALIZER = 16.0
HGRN_HEADS = 4
HGRN_DK = 128
HGRN_DV = 128
HGRN_K = HGRN_HEADS * HGRN_DK
HGRN_V = HGRN_HEADS * HGRN_DV
CHUNK = 64
D_FF = 4 * D_MODEL
N_MOD = 6
EPS = 1e-6

LANES = 128
SUBLANES = 8
RANK_PAD = LANES

OFF_GQ = 0
OFF_GK = OFF_GQ + GLA_QK
OFF_GV = OFF_GK + GLA_QK
OFF_LR = OFF_GV + GLA_V
OFF_GG = OFF_LR + RANK_PAD
OFF_HQ = OFF_GG + GLA_V
OFF_HF = OFF_HQ + HGRN_K
OFF_HI = OFF_HF + HGRN_K
OFF_HG = OFF_HI + HGRN_V
IN_WIDTH_P = OFF_HG + HGRN_V

TILE = 256
FF_BLOCK = 256
GROUP_FILL = (2, 1, 2)
FILL_UNITS_PER_MIX = 30
HEADS_PER_DOT = 2


def _rms(x):
    return x * lax.rsqrt(jnp.mean(x * x, axis=-1, keepdims=True) + EPS)


def _sigmoid(x):
    return 1.0 / (1.0 + jnp.exp(-x))


def _silu(x):
    return x * _sigmoid(x)


def _log_sigmoid(x):
    return jnp.minimum(x, 0.0) - jnp.log(1.0 + jnp.exp(-jnp.abs(x)))


def _mod_kernel(c_ref, w_ref, b_ref, o_ref):
    cond = _silu(c_ref[...])
    o_ref[...] = jnp.dot(cond.astype(BF16), w_ref[...].astype(BF16),
                         preferred_element_type=F32) + b_ref[...]


def _modulation(c, w_ada, b_ada):
    batch = c.shape[0]
    n_out = w_ada.shape[1]
    blk = D_MODEL
    return pl.pallas_call(
        _mod_kernel,
        grid=(n_out // blk,),
        in_specs=[
            pl.BlockSpec((batch, D_MODEL), lambda j: (0, 0)),
            pl.BlockSpec((D_MODEL, blk), lambda j: (0, j)),
            pl.BlockSpec((1, blk), lambda j: (0, j)),
        ],
        out_specs=pl.BlockSpec((batch, blk), lambda j: (0, j)),
        out_shape=jax.ShapeDtypeStruct((batch, n_out), F32),
        name="adaln_mod",
    )(c, w_ada, b_ada.reshape(1, n_out))


def _chunk_scan(x):
    row = lax.broadcasted_iota(jnp.int32, (SUBLANES, x.shape[1]), 0)
    out = []
    carry = None
    for g in range(x.shape[0] // SUBLANES):
        p = x[g * SUBLANES:(g + 1) * SUBLANES, :]
        s = 1
        while s < SUBLANES:
            p = p + jnp.where(row >= s, pltpu.roll(p, s, 0), 0.0)
            s *= 2
        if carry is not None:
            p = p + carry
        carry = p[SUBLANES - 1:SUBLANES, :]
        out.append(p)
    return jnp.concatenate(out, axis=0)


def _block_diag(blocks, n):
    zero = jnp.zeros_like(blocks[0])
    return jnp.concatenate(
        [jnp.concatenate([blocks[h] if g == h else zero for g in range(n)], axis=1)
         for h in range(n)], axis=0)


def _gated_linear_attention(q_fn, k_la_fn, v, s_ref, n_heads, dk, dv, fill):
    tile = v.shape[0]
    n_chunks = tile // CHUNK
    width = n_heads * dk
    n_lane_blocks = width // LANES
    fill_pre, fill_per_lane_block, fill_per_chunk = GROUP_FILL

    fill(fill_pre)
    q_rel = [[None] * n_lane_blocks for _ in range(n_chunks)]
    k_rel = [[None] * n_lane_blocks for _ in range(n_chunks)]
    k_end = [[None] * n_lane_blocks for _ in range(n_chunks)]
    q_in = [[None] * n_lane_blocks for _ in range(n_chunks)]
    last = [[None] * n_lane_blocks for _ in range(n_chunks)]
    for c in range(n_lane_blocks):
        cols = slice(c * LANES, (c + 1) * LANES)
        for j in range(n_chunks):
            rows = slice(j * CHUNK, (j + 1) * CHUNK)
            q = q_fn(rows, cols)
            k, log_a = k_la_fn(rows, cols)
            cum = _chunk_scan(log_a)
            cum_mid = cum[CHUNK // 2 - 1:CHUNK // 2, :]
            cum_last = cum[CHUNK - 1:CHUNK, :]
            q_rel[j][c] = (q * jnp.exp(cum - cum_mid)).astype(BF16)
            k_rel[j][c] = k * jnp.exp(cum_mid - cum)
            k_end[j][c] = k * jnp.exp(cum_last - cum)
            q_in[j][c] = (q * jnp.exp(cum)).astype(BF16)
            last[j][c] = cum_last
        fill(fill_per_lane_block)

    chunk_row = lax.broadcasted_iota(jnp.int32, (SUBLANES, width), 0)
    last_rows = jnp.zeros((SUBLANES, width), F32)
    for j in range(n_chunks):
        last_rows = jnp.where(chunk_row == j, jnp.concatenate(last[j], axis=1), last_rows)
    last_pad = jnp.concatenate([last_rows, jnp.zeros((LANES - SUBLANES, width), F32)], axis=0)
    decay_t = jnp.exp(last_pad.T)

    v_b = v.astype(BF16)
    hpd = HEADS_PER_DOT
    n_groups = n_heads // hpd
    gw = hpd * dk
    key_le_query = (lax.broadcasted_iota(jnp.int32, (CHUNK, hpd * CHUNK), 1) % CHUNK
                    <= lax.broadcasted_iota(jnp.int32, (CHUNK, hpd * CHUNK), 0))
    head_of_lane = lax.broadcasted_iota(jnp.int32, (CHUNK, gw), 1) // dk

    def group_scores(j, g):
        k_g = jnp.concatenate(k_rel[j], axis=1)[:, g * gw:(g + 1) * gw]
        if dk % LANES == 0:
            k_bd = _block_diag([k_g[:, b * dk:(b + 1) * dk] for b in range(hpd)], hpd)
        else:
            k_bd = jnp.concatenate(
                [jnp.where(head_of_lane == b, k_g, 0.0) for b in range(hpd)], axis=0)
        q_g = jnp.concatenate(q_rel[j], axis=1)[:, g * gw:(g + 1) * gw]
        return jnp.dot(q_g, k_bd.T.astype(BF16), preferred_element_type=F32)

    state = [s_ref[h] for h in range(n_heads)]
    out = []
    scores_next = [group_scores(0, g) for g in range(n_groups)]
    for j in range(n_chunks):
        rows = slice(j * CHUNK, (j + 1) * CHUNK)
        scores = scores_next
        if j + 1 < n_chunks:
            scores_next = [group_scores(j + 1, g) for g in range(n_groups)]
        fill(fill_per_chunk)
        q_in_j = jnp.concatenate(q_in[j], axis=1)
        out_j = []
        new_state = list(state)
        for g in range(n_groups):
            heads = range(g * hpd, (g + 1) * hpd)
            v_heads = [v_b[rows, h * dv:(h + 1) * dv] for h in heads]
            v_bd = _block_diag(v_heads, hpd)
            p = jnp.where(key_le_query, scores[g], 0.0).astype(BF16)
            w_state = _block_diag([state[h].astype(BF16) for h in heads], hpd)
            o_g = jnp.dot(jnp.concatenate([p, q_in_j[:, g * gw:(g + 1) * gw]], axis=1),
                          jnp.concatenate([v_bd, w_state], axis=0), preferred_element_type=F32)
            out_j.append(o_g)
            if dk % LANES == 0:
                k_t = jnp.concatenate([k_end[j][h] for h in heads], axis=0).T.astype(BF16)
                upd = jnp.dot(k_t, v_bd, preferred_element_type=F32)
                upd_heads = [upd[:, b * dv:(b + 1) * dv] for b in range(hpd)]
            else:
                k_t = k_end[j][g].T.astype(BF16)
                upd = jnp.dot(k_t, jnp.concatenate(v_heads, axis=1),
                              preferred_element_type=F32)
                upd_heads = [upd[b * dk:(b + 1) * dk, b * dv:(b + 1) * dv] for b in range(hpd)]
            for b, h in enumerate(heads):
                new_state[h] = decay_t[h * dk:(h + 1) * dk, j:j + 1] * state[h] + upd_heads[b]
        state = new_state
        out.append(jnp.concatenate(out_j, axis=1))
    for h in range(n_heads):
        s_ref[h] = state[h]
    out = jnp.concatenate(out, axis=0)
    return [out[:, h * dv:(h + 1) * dv] for h in range(n_heads)]


def _head_norm_gate(heads, gain, gate, dv):
    outs = [_rms(o_h) * gain * _silu(gate[:, h * dv:(h + 1) * dv]) for h, o_h in enumerate(heads)]
    return jnp.concatenate(outs, axis=1)


def _token_mixer(x, mod_ref, w_in_refs, w2_ref, b2_ref, gla_g_ref, lb_ref, hgrn_g_ref, w_out_ref,
                 s_gla_ref, s_hgrn_ref, fill, finish):
    shift = mod_ref[0, 0:1, :]
    scale = mod_ref[0, 1:2, :]
    gate = mod_ref[0, 2:3, :]
    fill(2)
    h_b = (_rms(x) * (1.0 + scale) + shift).astype(BF16)

    pieces = [jnp.dot(h_b, ref[...], preferred_element_type=F32) for ref in w_in_refs]

    def proj(off, width):
        for piece in pieces:
            if off < piece.shape[1]:
                return piece[:, off:off + width]
            off -= piece.shape[1]

    gate_logits = jnp.dot(proj(OFF_LR, RANK_PAD).astype(BF16), w2_ref[...],
                          preferred_element_type=F32) + b2_ref[...]
    gq = proj(OFF_GQ, GLA_QK)
    gk = proj(OFF_GK, GLA_QK)
    o_gla = _gated_linear_attention(
        lambda rows, cols: gq[rows, cols] * (GLA_DK ** -0.5),
        lambda rows, cols: (gk[rows, cols],
                            _log_sigmoid(gate_logits[rows, cols]) * (1.0 / GLA_GATE_NORMALIZER)),
        proj(OFF_GV, GLA_V), s_gla_ref, GLA_HEADS, GLA_DK, GLA_DV, fill)
    o_gla = _head_norm_gate(o_gla, gla_g_ref[...], proj(OFF_GG, GLA_V), GLA_DV)
    fill(2)

    lb_e = jnp.exp(lb_ref[...] - jnp.max(lb_ref[...], axis=0, keepdims=True))
    lb = lb_e[0:1, :] / jnp.sum(lb_e, axis=0, keepdims=True)
    hq = proj(OFF_HQ, HGRN_K)
    hf = proj(OFF_HF, HGRN_K)

    def hgrn_k_la(rows, cols):
        forget = lb[:, cols] + (1.0 - lb[:, cols]) * _sigmoid(hf[rows, cols])
        return 1.0 - forget, jnp.log(forget)

    o_hgrn = _gated_linear_attention(
        lambda rows, cols: _silu(hq[rows, cols]), hgrn_k_la, proj(OFF_HI, HGRN_V),
        s_hgrn_ref, HGRN_HEADS, HGRN_DK, HGRN_DV, fill)
    o_hgrn = _head_norm_gate(o_hgrn, hgrn_g_ref[...], proj(OFF_HG, HGRN_V), HGRN_DV)
    finish()

    mixed = jnp.concatenate([o_gla, o_hgrn], axis=1).astype(BF16)
    return x + gate * jnp.dot(mixed, w_out_ref[...], preferred_element_type=F32)


class _ChannelMixer:
    def __init__(self, x, mod_ref, w1_ref, w2_ref, g_ref):
        self.x, self.w1_ref, self.w2_ref, self.g_ref = x, w1_ref, w2_ref, g_ref
        shift = mod_ref[0, 3:4, :]
        scale = mod_ref[0, 4:5, :]
        self.gate = mod_ref[0, 5:6, :]
        self.h_b = (_rms(x) * (1.0 + scale) + shift).astype(BF16)
        self.acc = None
        self.hidden = []
        self.credit = 0.0
        self.up = 0
        self.down = 0

    def _up(self):
        lo = self.up * FF_BLOCK
        a = jnp.maximum(jnp.dot(self.h_b, self.w1_ref[:, lo:lo + FF_BLOCK],
                                preferred_element_type=F32), 0.0)
        self.hidden.append((a * a).astype(BF16))
        self.up += 1

    def _down(self):
        lo = self.down * FF_BLOCK
        y = jnp.dot(self.hidden.pop(0), self.w2_ref[lo:lo + FF_BLOCK, :],
                    preferred_element_type=F32)
        self.acc = y if self.acc is None else self.acc + y
        self.down += 1

    def step(self):
        n_blocks = D_FF // FF_BLOCK
        if self.up < n_blocks and self.up <= self.down + 1:
            self._up()
        elif self.down < n_blocks:
            self._down()

    def fill(self, n):
        self.credit += n * (2 * D_FF // FF_BLOCK) / FILL_UNITS_PER_MIX
        while self.credit >= 1.0:
            self.step()
            self.credit -= 1.0

    def finish(self):
        while self.down * FF_BLOCK < D_FF:
            self.step()
        return _rms(self.x + self.gate * self.acc) * self.g_ref[...]


def _layer_kernel(tiles_per_seq, x_ref, mod_a_ref, mod_b_ref, w_in_a_ref, w_in_b_ref, w2_ref, b2_ref,
                  gla_g_ref, lb_ref, hgrn_g_ref, w_out_ref, w_mlp1_ref, w_mlp2_ref, final_g_ref,
                  o_ref, s_gla_ref, s_hgrn_ref, x1_ref):
    i = pl.program_id(0)

    @pl.when(i == 0)
    def _():
        x1_ref[...] = jnp.zeros_like(x1_ref)

    @pl.when(i % tiles_per_seq == 0)
    def _():
        s_gla_ref[...] = jnp.zeros_like(s_gla_ref)
        s_hgrn_ref[...] = jnp.zeros_like(s_hgrn_ref)

    mlp = _ChannelMixer(x1_ref[(i + 1) % 2], mod_b_ref, w_mlp1_ref, w_mlp2_ref, final_g_ref)

    def finish():
        o_ref[0] = mlp.finish()

    x1_ref[i % 2] = _token_mixer(
        x_ref[0], mod_a_ref, (w_in_a_ref, w_in_b_ref), w2_ref, b2_ref, gla_g_ref, lb_ref,
        hgrn_g_ref, w_out_ref, s_gla_ref, s_hgrn_ref, mlp.fill, finish)


def _const_spec(shape):
    zeros = (0,) * len(shape)
    return pl.BlockSpec(shape, lambda *_: zeros, pipeline_mode=pl.Buffered(1))


def _nbytes(a):
    return a.size * a.dtype.itemsize


def _layer_vmem_limit(resident, tile_bytes):
    intermediates = 3 * TILE * IN_WIDTH_P * 4
    return sum(_nbytes(a) for a in resident) + 6 * tile_bytes + intermediates


def _layer(x, mod, w_in_parts, w2_p, b2, gla_g, lb_logits, hgrn_g, w_out_b, w_mlp1_b, w_mlp2_b,
           final_g):
    batch, seq, _ = x.shape
    tps = seq // TILE
    n_tiles = batch * tps

    def tile_in(i):
        j = jnp.minimum(i, n_tiles - 1)
        return (j // tps, j % tps, 0)

    def tile_out(i):
        j = jnp.maximum(i - 1, 0)
        return (j // tps, j % tps, 0)

    resident = (*w_in_parts, w2_p, b2, gla_g, lb_logits, hgrn_g, w_out_b, w_mlp1_b, w_mlp2_b,
                final_g)
    return pl.pallas_call(
        functools.partial(_layer_kernel, tps),
        grid=(n_tiles + 1,),
        in_specs=[
            pl.BlockSpec((1, TILE, D_MODEL), tile_in),
            pl.BlockSpec((1, N_MOD, D_MODEL), lambda i: (tile_in(i)[0], 0, 0)),
            pl.BlockSpec((1, N_MOD, D_MODEL), lambda i: (tile_out(i)[0], 0, 0)),
            *[_const_spec(a.shape) for a in resident],
        ],
        out_specs=pl.BlockSpec((1, TILE, D_MODEL), tile_out),
        out_shape=jax.ShapeDtypeStruct(x.shape, F32),
        scratch_shapes=[
            pltpu.VMEM((GLA_HEADS, GLA_DK, GLA_DV), F32),
            pltpu.VMEM((HGRN_HEADS, HGRN_DK, HGRN_DV), F32),
            pltpu.VMEM((2, TILE, D_MODEL), F32),
        ],
        compiler_params=pltpu.CompilerParams(
            dimension_semantics=("arbitrary",),
            vmem_limit_bytes=_layer_vmem_limit(resident, TILE * D_MODEL * 4)),
        name="hybrid_layer",
    )(x, mod, mod, *resident)


def _split_w_in(w_in):
    lr_hi = 2 * GLA_QK + GLA_V + GLA_GATE_RANK
    pad = jnp.zeros((w_in.shape[0], RANK_PAD - GLA_GATE_RANK), w_in.dtype)
    return (jnp.concatenate([w_in[:, :lr_hi], pad], axis=1).astype(BF16),
            w_in[:, lr_hi:].astype(BF16))


def kernel(x, c, w_ada, b_ada, w_in, gla_gate_w2, gla_gate_b2, gla_norm_g, hgrn_lb_logits,
           hgrn_norm_g, w_out, w_mlp1, w_mlp2, final_norm_g):
    assert w_ada.shape[0] == 1, "single-layer trunk"
    assert x.shape[1] % TILE == 0 and x.shape[2] == D_MODEL
    batch = x.shape[0]
    mod = _modulation(c, w_ada[0], b_ada[0]).reshape(batch, N_MOD, D_MODEL)

    w2_p = jnp.concatenate(
        [gla_gate_w2[0], jnp.zeros((RANK_PAD - GLA_GATE_RANK, GLA_QK), F32)], axis=0).astype(BF16)
    return _layer(x, mod, _split_w_in(w_in[0]), w2_p, gla_gate_b2[0].reshape(1, GLA_QK),
                  gla_norm_g[0].reshape(1, GLA_DV), hgrn_lb_logits,
                  hgrn_norm_g[0].reshape(1, HGRN_DV), w_out[0].astype(BF16),
                  w_mlp1[0].astype(BF16), w_mlp2[0].astype(BF16),
                  final_norm_g.reshape(1, D_MODEL))
```

```python
import functools

import jax
import jax.numpy as jnp
from jax import lax
from jax.experimental import pallas as pl
from jax.experimental.pallas import tpu as pltpu

F32 = jnp.float32
BF16 = jnp.bfloat16

D_MODEL = 1024
GLA_HEADS = 4
GLA_DK = 64
GLA_DV = 128
GLA_QK = GLA_HEADS * GLA_DK
GLA_V = GLA_HEADS * GLA_DV
GLA_GATE_RANK = 16
GLA_GATE_NORMALIZER = 16.0
HGRN_HEADS = 4
HGRN_DK = 128
HGRN_DV = 128
HGRN_K = HGRN_HEADS * HGRN_DK
HGRN_V = HGRN_HEADS * HGRN_DV
CHUNK = 64
D_FF = 4 * D_MODEL
N_MOD = 6
EPS = 1e-6

LANES = 128
SUBLANES = 8
RANK_PAD = LANES

OFF_GQ = 0
OFF_GK = OFF_GQ + GLA_QK
OFF_GV = OFF_GK + GLA_QK
OFF_LR = OFF_GV + GLA_V
OFF_GG = OFF_LR + RANK_PAD
OFF_HQ = OFF_GG + GLA_V
OFF_HF = OFF_HQ + HGRN_K
OFF_HI = OFF_HF + HGRN_K
OFF_HG = OFF_HI + HGRN_V
IN_WIDTH_P = OFF_HG + HGRN_V

TILE = 256
FF_BLOCK = 256
GROUP_FILL = (2, 1, 2)
FILL_UNITS_PER_MIX = 30
ROW_B2, ROW_GLA_G, ROW_HGRN_G, ROW_FINAL_G, ROW_LB = 0, 1, 2, 3, 4
HEADS_PER_DOT = 2


def _rms(x):
    return x * lax.rsqrt(jnp.mean(x * x, axis=-1, keepdims=True) + EPS)


def _sigmoid(x):
    return 1.0 / (1.0 + jnp.exp(-x))


def _silu(x):
    return x * _sigmoid(x)


def _log_sigmoid(x):
    return jnp.minimum(x, 0.0) - jnp.log(1.0 + jnp.exp(-jnp.abs(x)))


def _mod_kernel(c_ref, w_ref, b_ref, o_ref):
    cond = _silu(c_ref[...])
    o_ref[...] = jnp.dot(cond.astype(BF16), w_ref[...].astype(BF16),
                         preferred_element_type=F32) + b_ref[...]


def _modulation(c, w_ada, b_ada):
    batch = c.shape[0]
    n_out = w_ada.shape[1]
    blk = D_MODEL
    return pl.pallas_call(
        _mod_kernel,
        grid=(n_out // blk,),
        in_specs=[
            pl.BlockSpec((batch, D_MODEL), lambda j: (0, 0)),
            pl.BlockSpec((D_MODEL, blk), lambda j: (0, j)),
            pl.BlockSpec((1, blk), lambda j: (0, j)),
        ],
        out_specs=pl.BlockSpec((batch, blk), lambda j: (0, j)),
        out_shape=jax.ShapeDtypeStruct((batch, n_out), F32),
        name="adaln_mod",
    )(c, w_ada, b_ada.reshape(1, n_out))


def _chunk_scan(x):
    row = lax.broadcasted_iota(jnp.int32, (SUBLANES, x.shape[1]), 0)
    out = []
    carry = None
    for g in range(x.shape[0] // SUBLANES):
        p = x[g * SUBLANES:(g + 1) * SUBLANES, :]
        s = 1
        while s < SUBLANES:
            p = p + jnp.where(row >= s, pltpu.roll(p, s, 0), 0.0)
            s *= 2
        if carry is not None:
            p = p + carry
        carry = p[SUBLANES - 1:SUBLANES, :]
        out.append(p)
    return jnp.concatenate(out, axis=0)


def _block_diag(blocks, n):
    zero = jnp.zeros_like(blocks[0])
    return jnp.concatenate(
        [jnp.concatenate([blocks[h] if g == h else zero for g in range(n)], axis=1)
         for h in range(n)], axis=0)


def _gated_linear_attention(q_fn, k_la_fn, v, s_ref, n_heads, dk, dv, fill):
    tile = v.shape[0]
    n_chunks = tile // CHUNK
    width = n_heads * dk
    n_lane_blocks = width // LANES
    fill_pre, fill_per_lane_block, fill_per_chunk = GROUP_FILL

    fill(fill_pre)
    q_rel = [[None] * n_lane_blocks for _ in range(n_chunks)]
    k_rel = [[None] * n_lane_blocks for _ in range(n_chunks)]
    k_end = [[None] * n_lane_blocks for _ in range(n_chunks)]
    q_in = [[None] * n_lane_blocks for _ in range(n_chunks)]
    last = [[None] * n_lane_blocks for _ in range(n_chunks)]
    for c in range(n_lane_blocks):
        cols = slice(c * LANES, (c + 1) * LANES)
        for j in range(n_chunks):
            rows = slice(j * CHUNK, (j + 1) * CHUNK)
            q = q_fn(rows, cols)
            k, log_a = k_la_fn(rows, cols)
            cum = _chunk_scan(log_a)
            cum_mid = cum[CHUNK // 2 - 1:CHUNK // 2, :]
            cum_last = cum[CHUNK - 1:CHUNK, :]
            q_rel[j][c] = (q * jnp.exp(cum - cum_mid)).astype(BF16)
            k_rel[j][c] = k * jnp.exp(cum_mid - cum)
            k_end[j][c] = k * jnp.exp(cum_last - cum)
            q_in[j][c] = (q * jnp.exp(cum)).astype(BF16)
            last[j][c] = cum_last
        fill(fill_per_lane_block)

    chunk_row = lax.broadcasted_iota(jnp.int32, (SUBLANES, width), 0)
    last_rows = jnp.zeros((SUBLANES, width), F32)
    for j in range(n_chunks):
        last_rows = jnp.where(chunk_row == j, jnp.concatenate(last[j], axis=1), last_rows)
    last_pad = jnp.concatenate([last_rows, jnp.zeros((LANES - SUBLANES, width), F32)], axis=0)
    decay_t = jnp.exp(last_pad.T)

    v_b = v.astype(BF16)
    hpd = HEADS_PER_DOT
    n_groups = n_heads // hpd
    gw = hpd * dk
    key_le_query = (lax.broadcasted_iota(jnp.int32, (CHUNK, hpd * CHUNK), 1) % CHUNK
                    <= lax.broadcasted_iota(jnp.int32, (CHUNK, hpd * CHUNK), 0))
    head_of_lane = lax.broadcasted_iota(jnp.int32, (CHUNK, gw), 1) // dk

    def group_scores(j, g):
        k_g = jnp.concatenate(k_rel[j], axis=1)[:, g * gw:(g + 1) * gw]
        if dk % LANES == 0:
            k_bd = _block_diag([k_g[:, b * dk:(b + 1) * dk] for b in range(hpd)], hpd)
        else:
            k_bd = jnp.concatenate(
                [jnp.where(head_of_lane == b, k_g, 0.0) for b in range(hpd)], axis=0)
        q_g = jnp.concatenate(q_rel[j], axis=1)[:, g * gw:(g + 1) * gw]
        return jnp.dot(q_g, k_bd.T.astype(BF16), preferred_element_type=F32)

    state = [s_ref[h] for h in range(n_heads)]
    out = []
    scores_next = [group_scores(0, g) for g in range(n_groups)]
    for j in range(n_chunks):
        rows = slice(j * CHUNK, (j + 1) * CHUNK)
        scores = scores_next
        if j + 1 < n_chunks:
            scores_next = [group_scores(j + 1, g) for g in range(n_groups)]
        fill(fill_per_chunk)
        q_in_j = jnp.concatenate(q_in[j], axis=1)
        out_j = []
        new_state = list(state)
        for g in range(n_groups):
            heads = range(g * hpd, (g + 1) * hpd)
            v_heads = [v_b[rows, h * dv:(h + 1) * dv] for h in heads]
            v_bd = _block_diag(v_heads, hpd)
            p = jnp.where(key_le_query, scores[g], 0.0).astype(BF16)
            w_state = _block_diag([state[h].astype(BF16) for h in heads], hpd)
            o_g = jnp.dot(jnp.concatenate([p, q_in_j[:, g * gw:(g + 1) * gw]], axis=1),
                          jnp.concatenate([v_bd, w_state], axis=0), preferred_element_type=F32)
            out_j.append(o_g)
            if dk % LANES == 0:
                k_t = jnp.concatenate([k_end[j][h] for h in heads], axis=0).T.astype(BF16)
                upd = jnp.dot(k_t, v_bd, preferred_element_type=F32)
                upd_heads = [upd[:, b * dv:(b + 1) * dv] for b in range(hpd)]
            else:
                k_t = k_end[j][g].T.astype(BF16)
                upd = jnp.dot(k_t, jnp.concatenate(v_heads, axis=1),
                              preferred_element_type=F32)
                upd_heads = [upd[b * dk:(b + 1) * dk, b * dv:(b + 1) * dv] for b in range(hpd)]
            for b, h in enumerate(heads):
                new_state[h] = decay_t[h * dk:(h + 1) * dk, j:j + 1] * state[h] + upd_heads[b]
        state = new_state
        out.append(jnp.concatenate(out_j, axis=1))
    for h in range(n_heads):
        s_ref[h] = state[h]
    out = jnp.concatenate(out, axis=0)
    return [out[:, h * dv:(h + 1) * dv] for h in range(n_heads)]


def _head_norm_gate(heads, gain, gate, dv):
    outs = [_rms(o_h) * gain * _silu(gate[:, h * dv:(h + 1) * dv]) for h, o_h in enumerate(heads)]
    return jnp.concatenate(outs, axis=1)


def _token_mixer(x, mod_ref, w_in_refs, w2_ref, params_ref, w_out_ref, s_gla_ref, s_hgrn_ref,
                 start_filler, fill, finish):
    shift = mod_ref[0, 0:1, :]
    scale = mod_ref[0, 1:2, :]
    gate = mod_ref[0, 2:3, :]
    h_b = (_rms(x) * (1.0 + scale) + shift).astype(BF16)

    pieces = [jnp.dot(h_b, ref[...], preferred_element_type=F32) for ref in w_in_refs]
    start_filler()
    fill(2)

    def proj(off, width):
        for piece in pieces:
            if off < piece.shape[1]:
                return piece[:, off:off + width]
            off -= piece.shape[1]

    gate_logits = jnp.dot(proj(OFF_LR, RANK_PAD).astype(BF16), w2_ref[...],
                          preferred_element_type=F32) + params_ref[ROW_B2:ROW_B2 + 1, :GLA_QK]
    gq = proj(OFF_GQ, GLA_QK)
    gk = proj(OFF_GK, GLA_QK)
    o_gla = _gated_linear_attention(
        lambda rows, cols: gq[rows, cols] * (GLA_DK ** -0.5),
        lambda rows, cols: (gk[rows, cols],
                            _log_sigmoid(gate_logits[rows, cols]) * (1.0 / GLA_GATE_NORMALIZER)),
        proj(OFF_GV, GLA_V), s_gla_ref, GLA_HEADS, GLA_DK, GLA_DV, fill)
    o_gla = _head_norm_gate(o_gla, params_ref[ROW_GLA_G:ROW_GLA_G + 1, :GLA_DV], proj(OFF_GG, GLA_V),
                            GLA_DV)
    fill(2)

    lb_logits = params_ref[ROW_LB:ROW_LB + 2, :HGRN_K]
    lb_e = jnp.exp(lb_logits - jnp.max(lb_logits, axis=0, keepdims=True))
    lb = lb_e[0:1, :] / jnp.sum(lb_e, axis=0, keepdims=True)
    hq = proj(OFF_HQ, HGRN_K)
    hf = proj(OFF_HF, HGRN_K)

    def hgrn_k_la(rows, cols):
        forget = lb[:, cols] + (1.0 - lb[:, cols]) * _sigmoid(hf[rows, cols])
        return 1.0 - forget, jnp.log(forget)

    o_hgrn = _gated_linear_attention(
        lambda rows, cols: _silu(hq[rows, cols]), hgrn_k_la, proj(OFF_HI, HGRN_V),
        s_hgrn_ref, HGRN_HEADS, HGRN_DK, HGRN_DV, fill)
    o_hgrn = _head_norm_gate(o_hgrn, params_ref[ROW_HGRN_G:ROW_HGRN_G + 1, :HGRN_DV],
                             proj(OFF_HG, HGRN_V), HGRN_DV)
    finish()

    mixed = jnp.concatenate([o_gla, o_hgrn], axis=1).astype(BF16)
    return x + gate * jnp.dot(mixed, w_out_ref[...], preferred_element_type=F32)


class _ChannelMixer:
    def __init__(self, x, mod_ref, w1_ref, w2_ref, final_gain):
        self.x, self.w1_ref, self.w2_ref, self.final_gain = x, w1_ref, w2_ref, final_gain
        shift = mod_ref[0, 3:4, :]
        scale = mod_ref[0, 4:5, :]
        self.gate = mod_ref[0, 5:6, :]
        self.h_b = (_rms(x) * (1.0 + scale) + shift).astype(BF16)
        self.acc = None
        self.hidden = []
        self.credit = 0.0
        self.up = 0
        self.down = 0

    def _up(self):
        lo = self.up * FF_BLOCK
        a = jnp.maximum(jnp.dot(self.h_b, self.w1_ref[:, lo:lo + FF_BLOCK],
                                preferred_element_type=F32), 0.0)
        self.hidden.append((a * a).astype(BF16))
        self.up += 1

    def _down(self):
        lo = self.down * FF_BLOCK
        y = jnp.dot(self.hidden.pop(0), self.w2_ref[lo:lo + FF_BLOCK, :],
                    preferred_element_type=F32)
        self.acc = y if self.acc is None else self.acc + y
        self.down += 1

    def step(self):
        n_blocks = D_FF // FF_BLOCK
        if self.up < n_blocks and self.up <= self.down + 1:
            self._up()
        elif self.down < n_blocks:
            self._down()

    def fill(self, n):
        self.credit += n * (2 * D_FF // FF_BLOCK) / FILL_UNITS_PER_MIX
        while self.credit >= 1.0:
            self.step()
            self.credit -= 1.0

    def finish(self):
        while self.down * FF_BLOCK < D_FF:
            self.step()
        return _rms(self.x + self.gate * self.acc) * self.final_gain


def _layer_kernel(tiles_per_seq, x_ref, mod_a_ref, mod_b_ref, w_in_a_ref, w_in_b_ref, w2_ref,
                  params_ref, w_out_ref, w_mlp1_ref, w_mlp2_ref, o_ref, s_gla_ref, s_hgrn_ref, x1_ref):
    i = pl.program_id(0)

    @pl.when(i == 0)
    def _():
        x1_ref[...] = jnp.zeros_like(x1_ref)

    @pl.when(i % tiles_per_seq == 0)
    def _():
        s_gla_ref[...] = jnp.zeros_like(s_gla_ref)
        s_hgrn_ref[...] = jnp.zeros_like(s_hgrn_ref)

    channel_mixer = []

    def start_filler():
        channel_mixer.append(_ChannelMixer(x1_ref[(i + 1) % 2], mod_b_ref, w_mlp1_ref, w_mlp2_ref,
                                           params_ref[ROW_FINAL_G:ROW_FINAL_G + 1, :]))

    def fill(n):
        channel_mixer[0].fill(n)

    def finish():
        o_ref[0] = channel_mixer[0].finish()

    x1_ref[i % 2] = _token_mixer(
        x_ref[0], mod_a_ref, (w_in_a_ref, w_in_b_ref), w2_ref, params_ref, w_out_ref, s_gla_ref,
        s_hgrn_ref, start_filler, fill, finish)


def _const_spec(shape):
    zeros = (0,) * len(shape)
    return pl.BlockSpec(shape, lambda *_: zeros, pipeline_mode=pl.Buffered(1))


def _nbytes(a):
    return a.size * a.dtype.itemsize


def _layer_vmem_limit(resident, tile_bytes):
    intermediates = 3 * TILE * IN_WIDTH_P * 4
    return sum(_nbytes(a) for a in resident) + 6 * tile_bytes + intermediates


def _layer(x, mod, w_in_parts, w2_p, params, w_out_b, w_mlp1_b, w_mlp2_b):
    batch, seq, _ = x.shape
    tps = seq // TILE
    n_tiles = batch * tps

    def tile_in(i):
        j = jnp.minimum(i, n_tiles - 1)
        return (j // tps, j % tps, 0)

    def tile_out(i):
        j = jnp.maximum(i - 1, 0)
        return (j // tps, j % tps, 0)

    resident = (*w_in_parts, w2_p, params, w_out_b, w_mlp1_b, w_mlp2_b)
    return pl.pallas_call(
        functools.partial(_layer_kernel, tps),
        grid=(n_tiles + 1,),
        in_specs=[
            pl.BlockSpec((1, TILE, D_MODEL), tile_in),
            pl.BlockSpec((1, N_MOD, D_MODEL), lambda i: (tile_in(i)[0], 0, 0)),
            pl.BlockSpec((1, N_MOD, D_MODEL), lambda i: (tile_out(i)[0], 0, 0)),
            *[_const_spec(a.shape) for a in resident],
        ],
        out_specs=pl.BlockSpec((1, TILE, D_MODEL), tile_out),
        out_shape=jax.ShapeDtypeStruct(x.shape, F32),
        scratch_shapes=[
            pltpu.VMEM((GLA_HEADS, GLA_DK, GLA_DV), F32),
            pltpu.VMEM((HGRN_HEADS, HGRN_DK, HGRN_DV), F32),
            pltpu.VMEM((2, TILE, D_MODEL), F32),
        ],
        compiler_params=pltpu.CompilerParams(
            dimension_semantics=("arbitrary",),
            vmem_limit_bytes=_layer_vmem_limit(resident, TILE * D_MODEL * 4)),
        name="hybrid_layer",
    )(x, mod, mod, *resident)


def _split_w_in(w_in):
    lr_hi = 2 * GLA_QK + GLA_V + GLA_GATE_RANK
    pad = jnp.zeros((w_in.shape[0], RANK_PAD - GLA_GATE_RANK), w_in.dtype)
    return (jnp.concatenate([w_in[:, :lr_hi], pad], axis=1).astype(BF16),
            w_in[:, lr_hi:].astype(BF16))


def kernel(x, c, w_ada, b_ada, w_in, gla_gate_w2, gla_gate_b2, gla_norm_g, hgrn_lb_logits,
           hgrn_norm_g, w_out, w_mlp1, w_mlp2, final_norm_g):
    assert w_ada.shape[0] == 1, "single-layer trunk"
    assert x.shape[1] % TILE == 0 and x.shape[2] == D_MODEL
    batch = x.shape[0]
    mod = _modulation(c, w_ada[0], b_ada[0]).reshape(batch, N_MOD, D_MODEL)

    w2_p = jnp.concatenate(
        [gla_gate_w2[0], jnp.zeros((RANK_PAD - GLA_GATE_RANK, GLA_QK), F32)], axis=0).astype(BF16)
    def row(v):
        return jnp.pad(v.reshape(1, -1), ((0, 0), (0, D_MODEL - v.size)))

    params = jnp.concatenate(
        [row(gla_gate_b2[0]), row(gla_norm_g[0]), row(hgrn_norm_g[0]), row(final_norm_g),
         jnp.pad(hgrn_lb_logits, ((0, 0), (0, D_MODEL - HGRN_K))),
         jnp.zeros((SUBLANES - ROW_LB - 2, D_MODEL), F32)], axis=0)
    return _layer(x, mod, _split_w_in(w_in[0]), w2_p, params, w_out[0].astype(BF16),
                  w_mlp1[0].astype(BF16), w_mlp2[0].astype(BF16))
```

```python
import functools

import jax
import jax.numpy as jnp
from jax import lax
from jax.experimental import pallas as pl
from jax.experimental.pallas import tpu as pltpu

F32 = jnp.float32
BF16 = jnp.bfloat16

D_MODEL = 1024
GLA_HEADS = 4
GLA_DK = 64
GLA_DV = 128
GLA_QK = GLA_HEADS * GLA_DK
GLA_V = GLA_HEADS * GLA_DV
GLA_GATE_RANK = 16
GLA_GATE_NORMALIZER = 16.0
HGRN_HEADS = 4
HGRN_DK = 128
HGRN_DV = 128
HGRN_K = HGRN_HEADS * HGRN_DK
HGRN_V = HGRN_HEADS * HGRN_DV
CHUNK = 64
D_FF = 4 * D_MODEL
N_MOD = 6
EPS = 1e-6

LANES = 128
SUBLANES = 8
RANK_PAD = LANES

OFF_GQ = 0
OFF_GK = OFF_GQ + GLA_QK
OFF_GV = OFF_GK + GLA_QK
OFF_LR = OFF_GV + GLA_V
OFF_GG = OFF_LR + RANK_PAD
OFF_HQ = OFF_GG + GLA_V
OFF_HF = OFF_HQ + HGRN_K
OFF_HI = OFF_HF + HGRN_K
OFF_HG = OFF_HI + HGRN_V
IN_WIDTH_P = OFF_HG + HGRN_V

TILE = 256
FF_BLOCK = 256
GROUP_FILL = (2, 1, 2)
FILL_UNITS_PER_MIX = 30
ROW_B2, ROW_GLA_G, ROW_HGRN_G, ROW_FINAL_G, ROW_LB = 0, 1, 2, 3, 4
HEADS_PER_DOT = 2


def _rms(x):
    return x * lax.rsqrt(jnp.mean(x * x, axis=-1, keepdims=True) + EPS)


def _sigmoid(x):
    return 1.0 / (1.0 + jnp.exp(-x))


def _silu(x):
    return x * _sigmoid(x)


def _log_sigmoid(x):
    return jnp.minimum(x, 0.0) - jnp.log(1.0 + jnp.exp(-jnp.abs(x)))


def _mod_kernel(c_ref, w_ref, b_ref, o_ref):
    cond = _silu(c_ref[...])
    o_ref[...] = jnp.dot(cond.astype(BF16), w_ref[...].astype(BF16),
                         preferred_element_type=F32) + b_ref[...]


def _modulation(c, w_ada, b_ada):
    batch = c.shape[0]
    n_out = w_ada.shape[1]
    blk = D_MODEL
    return pl.pallas_call(
        _mod_kernel,
        grid=(n_out // blk,),
        in_specs=[
            pl.BlockSpec((batch, D_MODEL), lambda j: (0, 0)),
            pl.BlockSpec((D_MODEL, blk), lambda j: (0, j)),
            pl.BlockSpec((1, blk), lambda j: (0, j)),
        ],
        out_specs=pl.BlockSpec((batch, blk), lambda j: (0, j)),
        out_shape=jax.ShapeDtypeStruct((batch, n_out), F32),
        name="adaln_mod",
    )(c, w_ada, b_ada.reshape(1, n_out))


def _chunk_scan(x):
    row = lax.broadcasted_iota(jnp.int32, (SUBLANES, x.shape[1]), 0)
    out = []
    carry = None
    for g in range(x.shape[0] // SUBLANES):
        p = x[g * SUBLANES:(g + 1) * SUBLANES, :]
        s = 1
        while s < SUBLANES:
            p = p + jnp.where(row >= s, pltpu.roll(p, s, 0), 0.0)
            s *= 2
        if carry is not None:
            p = p + carry
        carry = p[SUBLANES - 1:SUBLANES, :]
        out.append(p)
    return jnp.concatenate(out, axis=0)


def _block_diag(blocks, n):
    zero = jnp.zeros_like(blocks[0])
    return jnp.concatenate(
        [jnp.concatenate([blocks[h] if g == h else zero for g in range(n)], axis=1)
         for h in range(n)], axis=0)


def _gated_linear_attention(q_fn, k_la_fn, v, s_ref, n_heads, dk, dv, fill):
    tile = v.shape[0]
    n_chunks = tile // CHUNK
    width = n_heads * dk
    n_lane_blocks = width // LANES
    fill_pre, fill_per_lane_block, fill_per_chunk = GROUP_FILL

    fill(fill_pre)
    q_rel = [[None] * n_lane_blocks for _ in range(n_chunks)]
    k_rel = [[None] * n_lane_blocks for _ in range(n_chunks)]
    k_end = [[None] * n_lane_blocks for _ in range(n_chunks)]
    q_in = [[None] * n_lane_blocks for _ in range(n_chunks)]
    last = [[None] * n_lane_blocks for _ in range(n_chunks)]
    for c in range(n_lane_blocks):
        cols = slice(c * LANES, (c + 1) * LANES)
        for j in range(n_chunks):
            rows = slice(j * CHUNK, (j + 1) * CHUNK)
            q = q_fn(rows, cols)
            k, log_a = k_la_fn(rows, cols)
            cum = _chunk_scan(log_a)
            cum_mid = cum[CHUNK // 2 - 1:CHUNK // 2, :]
            cum_last = cum[CHUNK - 1:CHUNK, :]
            q_rel[j][c] = (q * jnp.exp(cum - cum_mid)).astype(BF16)
            k_rel[j][c] = k * jnp.exp(cum_mid - cum)
            k_end[j][c] = k * jnp.exp(cum_last - cum)
            q_in[j][c] = (q * jnp.exp(cum)).astype(BF16)
            last[j][c] = cum_last
        fill(fill_per_lane_block)

    chunk_row = lax.broadcasted_iota(jnp.int32, (SUBLANES, width), 0)
    last_rows = jnp.zeros((SUBLANES, width), F32)
    for j in range(n_chunks):
        last_rows = jnp.where(chunk_row == j, jnp.concatenate(last[j], axis=1), last_rows)
    last_pad = jnp.concatenate([last_rows, jnp.zeros((LANES - SUBLANES, width), F32)], axis=0)
    decay_t = jnp.exp(last_pad.T)

    yield
    v_b = v.astype(BF16)
    hpd = HEADS_PER_DOT
    n_groups = n_heads // hpd
    gw = hpd * dk
    key_le_query = (lax.broadcasted_iota(jnp.int32, (CHUNK, hpd * CHUNK), 1) % CHUNK
                    <= lax.broadcasted_iota(jnp.int32, (CHUNK, hpd * CHUNK), 0))
    head_of_lane = lax.broadcasted_iota(jnp.int32, (CHUNK, gw), 1) // dk

    def group_scores(j, g):
        k_g = jnp.concatenate(k_rel[j], axis=1)[:, g * gw:(g + 1) * gw]
        if dk % LANES == 0:
            k_bd = _block_diag([k_g[:, b * dk:(b + 1) * dk] for b in range(hpd)], hpd)
        else:
            k_bd = jnp.concatenate(
                [jnp.where(head_of_lane == b, k_g, 0.0) for b in range(hpd)], axis=0)
        q_g = jnp.concatenate(q_rel[j], axis=1)[:, g * gw:(g + 1) * gw]
        return jnp.dot(q_g, k_bd.T.astype(BF16), preferred_element_type=F32)

    state = [s_ref[h] for h in range(n_heads)]
    out = []
    scores_next = [group_scores(0, g) for g in range(n_groups)]
    for j in range(n_chunks):
        rows = slice(j * CHUNK, (j + 1) * CHUNK)
        scores = scores_next
        if j + 1 < n_chunks:
            scores_next = [group_scores(j + 1, g) for g in range(n_groups)]
        fill(fill_per_chunk)
        q_in_j = jnp.concatenate(q_in[j], axis=1)
        out_j = []
        new_state = list(state)
        for g in range(n_groups):
            heads = range(g * hpd, (g + 1) * hpd)
            v_heads = [v_b[rows, h * dv:(h + 1) * dv] for h in heads]
            v_bd = _block_diag(v_heads, hpd)
            p = jnp.where(key_le_query, scores[g], 0.0).astype(BF16)
            w_state = _block_diag([state[h].astype(BF16) for h in heads], hpd)
            o_g = jnp.dot(jnp.concatenate([p, q_in_j[:, g * gw:(g + 1) * gw]], axis=1),
                          jnp.concatenate([v_bd, w_state], axis=0), preferred_element_type=F32)
            out_j.append(o_g)
            if dk % LANES == 0:
                k_t = jnp.concatenate([k_end[j][h] for h in heads], axis=0).T.astype(BF16)
                upd = jnp.dot(k_t, v_bd, preferred_element_type=F32)
                upd_heads = [upd[:, b * dv:(b + 1) * dv] for b in range(hpd)]
            else:
                k_t = k_end[j][g].T.astype(BF16)
                upd = jnp.dot(k_t, jnp.concatenate(v_heads, axis=1),
                              preferred_element_type=F32)
                upd_heads = [upd[b * dk:(b + 1) * dk, b * dv:(b + 1) * dv] for b in range(hpd)]
            for b, h in enumerate(heads):
                new_state[h] = decay_t[h * dk:(h + 1) * dk, j:j + 1] * state[h] + upd_heads[b]
        state = new_state
        out.append(jnp.concatenate(out_j, axis=1))
        yield
    for h in range(n_heads):
        s_ref[h] = state[h]
    out = jnp.concatenate(out, axis=0)
    return [out[:, h * dv:(h + 1) * dv] for h in range(n_heads)]


def _interleave(generators):
    results = [None] * len(generators)
    live = list(range(len(generators)))
    while live:
        for idx in list(live):
            try:
                next(generators[idx])
            except StopIteration as done:
                results[idx] = done.value
                live.remove(idx)
    return results


def _head_norm_gate(heads, gain, gate, dv):
    outs = [_rms(o_h) * gain * _silu(gate[:, h * dv:(h + 1) * dv]) for h, o_h in enumerate(heads)]
    return jnp.concatenate(outs, axis=1)


def _token_mixer(x, mod_ref, w_in_refs, w2_ref, params_ref, w_out_ref, s_gla_ref, s_hgrn_ref,
                 start_filler, fill, finish):
    shift = mod_ref[0, 0:1, :]
    scale = mod_ref[0, 1:2, :]
    gate = mod_ref[0, 2:3, :]
    h_b = (_rms(x) * (1.0 + scale) + shift).astype(BF16)

    pieces = [jnp.dot(h_b, ref[...], preferred_element_type=F32) for ref in w_in_refs]
    start_filler()
    fill(2)

    def proj(off, width):
        for piece in pieces:
            if off < piece.shape[1]:
                return piece[:, off:off + width]
            off -= piece.shape[1]

    gate_logits = jnp.dot(proj(OFF_LR, RANK_PAD).astype(BF16), w2_ref[...],
                          preferred_element_type=F32) + params_ref[ROW_B2:ROW_B2 + 1, :GLA_QK]
    gq = proj(OFF_GQ, GLA_QK)
    gk = proj(OFF_GK, GLA_QK)
    gla = _gated_linear_attention(
        lambda rows, cols: gq[rows, cols] * (GLA_DK ** -0.5),
        lambda rows, cols: (gk[rows, cols],
                            _log_sigmoid(gate_logits[rows, cols]) * (1.0 / GLA_GATE_NORMALIZER)),
        proj(OFF_GV, GLA_V), s_gla_ref, GLA_HEADS, GLA_DK, GLA_DV, fill)

    lb_logits = params_ref[ROW_LB:ROW_LB + 2, :HGRN_K]
    lb_e = jnp.exp(lb_logits - jnp.max(lb_logits, axis=0, keepdims=True))
    lb = lb_e[0:1, :] / jnp.sum(lb_e, axis=0, keepdims=True)
    hq = proj(OFF_HQ, HGRN_K)
    hf = proj(OFF_HF, HGRN_K)

    def hgrn_k_la(rows, cols):
        forget = lb[:, cols] + (1.0 - lb[:, cols]) * _sigmoid(hf[rows, cols])
        return 1.0 - forget, jnp.log(forget)

    hgrn = _gated_linear_attention(
        lambda rows, cols: _silu(hq[rows, cols]), hgrn_k_la, proj(OFF_HI, HGRN_V),
        s_hgrn_ref, HGRN_HEADS, HGRN_DK, HGRN_DV, fill)

    o_gla, o_hgrn = _interleave([gla, hgrn])
    o_gla = _head_norm_gate(o_gla, params_ref[ROW_GLA_G:ROW_GLA_G + 1, :GLA_DV], proj(OFF_GG, GLA_V),
                            GLA_DV)
    fill(2)
    o_hgrn = _head_norm_gate(o_hgrn, params_ref[ROW_HGRN_G:ROW_HGRN_G + 1, :HGRN_DV],
                             proj(OFF_HG, HGRN_V), HGRN_DV)
    finish()

    mixed = jnp.concatenate([o_gla, o_hgrn], axis=1).astype(BF16)
    return x + gate * jnp.dot(mixed, w_out_ref[...], preferred_element_type=F32)


class _ChannelMixer:
    def __init__(self, x, mod_ref, w1_ref, w2_ref, final_gain):
        self.x, self.w1_ref, self.w2_ref, self.final_gain = x, w1_ref, w2_ref, final_gain
        shift = mod_ref[0, 3:4, :]
        scale = mod_ref[0, 4:5, :]
        self.gate = mod_ref[0, 5:6, :]
        self.h_b = (_rms(x) * (1.0 + scale) + shift).astype(BF16)
        self.acc = None
        self.hidden = []
        self.credit = 0.0
        self.up = 0
        self.down = 0

    def _up(self):
        lo = self.up * FF_BLOCK
        a = jnp.maximum(jnp.dot(self.h_b, self.w1_ref[:, lo:lo + FF_BLOCK],
                                preferred_element_type=F32), 0.0)
        self.hidden.append((a * a).astype(BF16))
        self.up += 1

    def _down(self):
        lo = self.down * FF_BLOCK
        y = jnp.dot(self.hidden.pop(0), self.w2_ref[lo:lo + FF_BLOCK, :],
                    preferred_element_type=F32)
        self.acc = y if self.acc is None else self.acc + y
        self.down += 1

    def step(self):
        n_blocks = D_FF // FF_BLOCK
        if self.up < n_blocks and self.up <= self.down + 1:
            self._up()
        elif self.down < n_blocks:
            self._down()

    def fill(self, n):
        self.credit += n * (2 * D_FF // FF_BLOCK) / FILL_UNITS_PER_MIX
        while self.credit >= 1.0:
            self.step()
            self.credit -= 1.0

    def finish(self):
        while self.down * FF_BLOCK < D_FF:
            self.step()
        return _rms(self.x + self.gate * self.acc) * self.final_gain


def _layer_kernel(tiles_per_seq, x_ref, mod_a_ref, mod_b_ref, w_in_a_ref, w_in_b_ref, w2_ref,
                  params_ref, w_out_ref, w_mlp1_ref, w_mlp2_ref, o_ref, s_gla_ref, s_hgrn_ref, x1_ref):
    i = pl.program_id(0)

    @pl.when(i == 0)
    def _():
        x1_ref[...] = jnp.zeros_like(x1_ref)

    @pl.when(i % tiles_per_seq == 0)
    def _():
        s_gla_ref[...] = jnp.zeros_like(s_gla_ref)
        s_hgrn_ref[...] = jnp.zeros_like(s_hgrn_ref)

    channel_mixer = []

    def start_filler():
        channel_mixer.append(_ChannelMixer(x1_ref[(i + 1) % 2], mod_b_ref, w_mlp1_ref, w_mlp2_ref,
                                           params_ref[ROW_FINAL_G:ROW_FINAL_G + 1, :]))

    def fill(n):
        channel_mixer[0].fill(n)

    def finish():
        o_ref[0] = channel_mixer[0].finish()

    x1_ref[i % 2] = _token_mixer(
        x_ref[0], mod_a_ref, (w_in_a_ref, w_in_b_ref), w2_ref, params_ref, w_out_ref, s_gla_ref,
        s_hgrn_ref, start_filler, fill, finish)


def _const_spec(shape):
    zeros = (0,) * len(shape)
    return pl.BlockSpec(shape, lambda *_: zeros, pipeline_mode=pl.Buffered(1))


def _nbytes(a):
    return a.size * a.dtype.itemsize


def _layer_vmem_limit(resident, tile_bytes):
    intermediates = 3 * TILE * IN_WIDTH_P * 4
    return sum(_nbytes(a) for a in resident) + 6 * tile_bytes + intermediates


def _layer(x, mod, w_in_parts, w2_p, params, w_out_b, w_mlp1_b, w_mlp2_b):
    batch, seq, _ = x.shape
    tps = seq // TILE
    n_tiles = batch * tps

    def tile_in(i):
        j = jnp.minimum(i, n_tiles - 1)
        return (j // tps, j % tps, 0)

    def tile_out(i):
        j = jnp.maximum(i - 1, 0)
        return (j // tps, j % tps, 0)

    resident = (*w_in_parts, w2_p, params, w_out_b, w_mlp1_b, w_mlp2_b)
    return pl.pallas_call(
        functools.partial(_layer_kernel, tps),
        grid=(n_tiles + 1,),
        in_specs=[
            pl.BlockSpec((1, TILE, D_MODEL), tile_in),
            pl.BlockSpec((1, N_MOD, D_MODEL), lambda i: (tile_in(i)[0], 0, 0)),
            pl.BlockSpec((1, N_MOD, D_MODEL), lambda i: (tile_out(i)[0], 0, 0)),
            *[_const_spec(a.shape) for a in resident],
        ],
        out_specs=pl.BlockSpec((1, TILE, D_MODEL), tile_out),
        out_shape=jax.ShapeDtypeStruct(x.shape, F32),
        scratch_shapes=[
            pltpu.VMEM((GLA_HEADS, GLA_DK, GLA_DV), F32),
            pltpu.VMEM((HGRN_HEADS, HGRN_DK, HGRN_DV), F32),
            pltpu.VMEM((2, TILE, D_MODEL), F32),
        ],
        compiler_params=pltpu.CompilerParams(
            dimension_semantics=("arbitrary",),
            vmem_limit_bytes=_layer_vmem_limit(resident, TILE * D_MODEL * 4)),
        name="hybrid_layer",
    )(x, mod, mod, *resident)


def _split_w_in(w_in):
    lr_hi = 2 * GLA_QK + GLA_V + GLA_GATE_RANK
    pad = jnp.zeros((w_in.shape[0], RANK_PAD - GLA_GATE_RANK), w_in.dtype)
    return (jnp.concatenate([w_in[:, :lr_hi], pad], axis=1).astype(BF16),
            w_in[:, lr_hi:].astype(BF16))


def kernel(x, c, w_ada, b_ada, w_in, gla_gate_w2, gla_gate_b2, gla_norm_g, hgrn_lb_logits,
           hgrn_norm_g, w_out, w_mlp1, w_mlp2, final_norm_g):
    assert w_ada.shape[0] == 1, "single-layer trunk"
    assert x.shape[1] % TILE == 0 and x.shape[2] == D_MODEL
    batch = x.shape[0]
    mod = _modulation(c, w_ada[0], b_ada[0]).reshape(batch, N_MOD, D_MODEL)

    w2_p = jnp.concatenate(
        [gla_gate_w2[0], jnp.zeros((RANK_PAD - GLA_GATE_RANK, GLA_QK), F32)], axis=0).astype(BF16)
    def row(v):
        return jnp.pad(v.reshape(1, -1), ((0, 0), (0, D_MODEL - v.size)))

    params = jnp.concatenate(
        [row(gla_gate_b2[0]), row(gla_norm_g[0]), row(hgrn_norm_g[0]), row(final_norm_g),
         jnp.pad(hgrn_lb_logits, ((0, 0), (0, D_MODEL - HGRN_K))),
         jnp.zeros((SUBLANES - ROW_LB - 2, D_MODEL), F32)], axis=0)
    return _layer(x, mod, _split_w_in(w_in[0]), w2_p, params, w_out[0].astype(BF16),
                  w_mlp1[0].astype(BF16), w_mlp2[0].astype(BF16))
```

```python
import functools

import jax
import jax.numpy as jnp
from jax import lax
from jax.experimental import pallas as pl
from jax.experimental.pallas import tpu as pltpu

F32 = jnp.float32
BF16 = jnp.bfloat16

D_MODEL = 1024
GLA_HEADS = 4
GLA_DK = 64
GLA_DV = 128
GLA_QK = GLA_HEADS * GLA_DK
GLA_V = GLA_HEADS * GLA_DV
GLA_GATE_RANK = 16
GLA_GATE_NORMALIZER = 16.0
HGRN_HEADS = 4
HGRN_DK = 128
HGRN_DV = 128
HGRN_K = HGRN_HEADS * HGRN_DK
HGRN_V = HGRN_HEADS * HGRN_DV
CHUNK = 64
D_FF = 4 * D_MODEL
N_MOD = 6
EPS = 1e-6

LANES = 128
SUBLANES = 8
RANK_PAD = LANES

OFF_GQ = 0
OFF_GK = OFF_GQ + GLA_QK
OFF_GV = OFF_GK + GLA_QK
OFF_LR = OFF_GV + GLA_V
OFF_GG = OFF_LR + RANK_PAD
OFF_HQ = OFF_GG + GLA_V
OFF_HF = OFF_HQ + HGRN_K
OFF_HI = OFF_HF + HGRN_K
OFF_HG = OFF_HI + HGRN_V
IN_WIDTH_P = OFF_HG + HGRN_V

TILE = 256
FF_BLOCK = 256
GROUP_FILL = (2, 1, 2)
FILL_UNITS_PER_MIX = 30
ROW_B2, ROW_GLA_G, ROW_HGRN_G, ROW_FINAL_G, ROW_LB = 0, 1, 2, 3, 4
CAST_CHUNK_BYTES = 2 * 1024 * 1024
HEADS_PER_DOT = 2


def _rms(x):
    return x * lax.rsqrt(jnp.mean(x * x, axis=-1, keepdims=True) + EPS)


def _sigmoid(x):
    return 1.0 / (1.0 + jnp.exp(-x))


def _silu(x):
    return x * _sigmoid(x)


def _log_sigmoid(x):
    return jnp.minimum(x, 0.0) - jnp.log(1.0 + jnp.exp(-jnp.abs(x)))


def _mod_kernel(c_ref, w_ref, b_ref, o_ref):
    cond = _silu(c_ref[...])
    o_ref[...] = jnp.dot(cond.astype(BF16), w_ref[...].astype(BF16),
                         preferred_element_type=F32) + b_ref[...]


def _modulation(c, w_ada, b_ada):
    batch = c.shape[0]
    n_out = w_ada.shape[1]
    blk = D_MODEL
    return pl.pallas_call(
        _mod_kernel,
        grid=(n_out // blk,),
        in_specs=[
            pl.BlockSpec((batch, D_MODEL), lambda j: (0, 0)),
            pl.BlockSpec((D_MODEL, blk), lambda j: (0, j)),
            pl.BlockSpec((1, blk), lambda j: (0, j)),
        ],
        out_specs=pl.BlockSpec((batch, blk), lambda j: (0, j)),
        out_shape=jax.ShapeDtypeStruct((batch, n_out), F32),
        name="adaln_mod",
    )(c, w_ada, b_ada.reshape(1, n_out))


def _chunk_scan(x):
    row = lax.broadcasted_iota(jnp.int32, (SUBLANES, x.shape[1]), 0)
    out = []
    carry = None
    for g in range(x.shape[0] // SUBLANES):
        p = x[g * SUBLANES:(g + 1) * SUBLANES, :]
        s = 1
        while s < SUBLANES:
            p = p + jnp.where(row >= s, pltpu.roll(p, s, 0), 0.0)
            s *= 2
        if carry is not None:
            p = p + carry
        carry = p[SUBLANES - 1:SUBLANES, :]
        out.append(p)
    return jnp.concatenate(out, axis=0)


def _block_diag(blocks, n):
    zero = jnp.zeros_like(blocks[0])
    return jnp.concatenate(
        [jnp.concatenate([blocks[h] if g == h else zero for g in range(n)], axis=1)
         for h in range(n)], axis=0)


def _gated_linear_attention(q_fn, k_la_fn, v, s_ref, n_heads, dk, dv, fill):
    tile = v.shape[0]
    n_chunks = tile // CHUNK
    width = n_heads * dk
    n_lane_blocks = width // LANES
    fill_pre, fill_per_lane_block, fill_per_chunk = GROUP_FILL

    fill(fill_pre)
    q_rel = [[None] * n_lane_blocks for _ in range(n_chunks)]
    k_rel = [[None] * n_lane_blocks for _ in range(n_chunks)]
    k_end = [[None] * n_lane_blocks for _ in range(n_chunks)]
    q_in = [[None] * n_lane_blocks for _ in range(n_chunks)]
    last = [[None] * n_lane_blocks for _ in range(n_chunks)]
    for c in range(n_lane_blocks):
        cols = slice(c * LANES, (c + 1) * LANES)
        for j in range(n_chunks):
            rows = slice(j * CHUNK, (j + 1) * CHUNK)
            q = q_fn(rows, cols)
            k, log_a = k_la_fn(rows, cols)
            cum = _chunk_scan(log_a)
            cum_mid = cum[CHUNK // 2 - 1:CHUNK // 2, :]
            cum_last = cum[CHUNK - 1:CHUNK, :]
            q_rel[j][c] = (q * jnp.exp(cum - cum_mid)).astype(BF16)
            k_rel[j][c] = k * jnp.exp(cum_mid - cum)
            k_end[j][c] = k * jnp.exp(cum_last - cum)
            q_in[j][c] = (q * jnp.exp(cum)).astype(BF16)
            last[j][c] = cum_last
        fill(fill_per_lane_block)

    chunk_row = lax.broadcasted_iota(jnp.int32, (SUBLANES, width), 0)
    last_rows = jnp.zeros((SUBLANES, width), F32)
    for j in range(n_chunks):
        last_rows = jnp.where(chunk_row == j, jnp.concatenate(last[j], axis=1), last_rows)
    last_pad = jnp.concatenate([last_rows, jnp.zeros((LANES - SUBLANES, width), F32)], axis=0)
    decay_t = jnp.exp(last_pad.T)

    yield
    v_b = v.astype(BF16)
    hpd = HEADS_PER_DOT
    n_groups = n_heads // hpd
    gw = hpd * dk
    key_le_query = (lax.broadcasted_iota(jnp.int32, (CHUNK, hpd * CHUNK), 1) % CHUNK
                    <= lax.broadcasted_iota(jnp.int32, (CHUNK, hpd * CHUNK), 0))
    head_of_lane = lax.broadcasted_iota(jnp.int32, (CHUNK, gw), 1) // dk

    def group_scores(j, g):
        k_g = jnp.concatenate(k_rel[j], axis=1)[:, g * gw:(g + 1) * gw]
        if dk % LANES == 0:
            k_bd = _block_diag([k_g[:, b * dk:(b + 1) * dk] for b in range(hpd)], hpd)
        else:
            k_bd = jnp.concatenate(
                [jnp.where(head_of_lane == b, k_g, 0.0) for b in range(hpd)], axis=0)
        q_g = jnp.concatenate(q_rel[j], axis=1)[:, g * gw:(g + 1) * gw]
        return jnp.dot(q_g, k_bd.T.astype(BF16), preferred_element_type=F32)

    state = [s_ref[h] for h in range(n_heads)]
    out = []
    scores_next = [group_scores(0, g) for g in range(n_groups)]
    for j in range(n_chunks):
        rows = slice(j * CHUNK, (j + 1) * CHUNK)
        scores = scores_next
        if j + 1 < n_chunks:
            scores_next = [group_scores(j + 1, g) for g in range(n_groups)]
        fill(fill_per_chunk)
        q_in_j = jnp.concatenate(q_in[j], axis=1)
        out_j = []
        new_state = list(state)
        for g in range(n_groups):
            heads = range(g * hpd, (g + 1) * hpd)
            v_heads = [v_b[rows, h * dv:(h + 1) * dv] for h in heads]
            v_bd = _block_diag(v_heads, hpd)
            p = jnp.where(key_le_query, scores[g], 0.0).astype(BF16)
            w_state = _block_diag([state[h].astype(BF16) for h in heads], hpd)
            o_g = jnp.dot(jnp.concatenate([p, q_in_j[:, g * gw:(g + 1) * gw]], axis=1),
                          jnp.concatenate([v_bd, w_state], axis=0), preferred_element_type=F32)
            out_j.append(o_g)
            if dk % LANES == 0:
                k_t = jnp.concatenate([k_end[j][h] for h in heads], axis=0).T.astype(BF16)
                upd = jnp.dot(k_t, v_bd, preferred_element_type=F32)
                upd_heads = [upd[:, b * dv:(b + 1) * dv] for b in range(hpd)]
            else:
                k_t = k_end[j][g].T.astype(BF16)
                upd = jnp.dot(k_t, jnp.concatenate(v_heads, axis=1),
                              preferred_element_type=F32)
                upd_heads = [upd[b * dk:(b + 1) * dk, b * dv:(b + 1) * dv] for b in range(hpd)]
            for b, h in enumerate(heads):
                new_state[h] = decay_t[h * dk:(h + 1) * dk, j:j + 1] * state[h] + upd_heads[b]
        state = new_state
        out.append(jnp.concatenate(out_j, axis=1))
        yield
    for h in range(n_heads):
        s_ref[h] = state[h]
    out = jnp.concatenate(out, axis=0)
    return [out[:, h * dv:(h + 1) * dv] for h in range(n_heads)]


def _interleave(generators):
    results = [None] * len(generators)
    live = list(range(len(generators)))
    while live:
        for idx in list(live):
            try:
                next(generators[idx])
            except StopIteration as done:
                results[idx] = done.value
                live.remove(idx)
    return results


def _head_norm_gate(heads, gain, gate, dv):
    outs = [_rms(o_h) * gain * _silu(gate[:, h * dv:(h + 1) * dv]) for h, o_h in enumerate(heads)]
    return jnp.concatenate(outs, axis=1)


def _token_mixer(x, mod_ref, w_in_refs, w2_ref, params_ref, w_out_ref, s_gla_ref, s_hgrn_ref,
                 start_filler, fill, finish):
    shift = mod_ref[0, 0:1, :]
    scale = mod_ref[0, 1:2, :]
    gate = mod_ref[0, 2:3, :]
    h_b = (_rms(x) * (1.0 + scale) + shift).astype(BF16)

    pieces = [jnp.dot(h_b, ref[...], preferred_element_type=F32) for ref in w_in_refs]
    start_filler()
    fill(2)

    def proj(off, width):
        for piece in pieces:
            if off < piece.shape[1]:
                return piece[:, off:off + width]
            off -= piece.shape[1]

    gate_logits = jnp.dot(proj(OFF_LR, RANK_PAD).astype(BF16), w2_ref[...],
                          preferred_element_type=F32) + params_ref[ROW_B2:ROW_B2 + 1, :GLA_QK]
    gq = proj(OFF_GQ, GLA_QK)
    gk = proj(OFF_GK, GLA_QK)
    gla = _gated_linear_attention(
        lambda rows, cols: gq[rows, cols] * (GLA_DK ** -0.5),
        lambda rows, cols: (gk[rows, cols],
                            _log_sigmoid(gate_logits[rows, cols]) * (1.0 / GLA_GATE_NORMALIZER)),
        proj(OFF_GV, GLA_V), s_gla_ref, GLA_HEADS, GLA_DK, GLA_DV, fill)

    lb_logits = params_ref[ROW_LB:ROW_LB + 2, :HGRN_K]
    lb_e = jnp.exp(lb_logits - jnp.max(lb_logits, axis=0, keepdims=True))
    lb = lb_e[0:1, :] / jnp.sum(lb_e, axis=0, keepdims=True)
    hq = proj(OFF_HQ, HGRN_K)
    hf = proj(OFF_HF, HGRN_K)

    def hgrn_k_la(rows, cols):
        forget = lb[:, cols] + (1.0 - lb[:, cols]) * _sigmoid(hf[rows, cols])
        return 1.0 - forget, jnp.log(forget)

    hgrn = _gated_linear_attention(
        lambda rows, cols: _silu(hq[rows, cols]), hgrn_k_la, proj(OFF_HI, HGRN_V),
        s_hgrn_ref, HGRN_HEADS, HGRN_DK, HGRN_DV, fill)

    o_gla, o_hgrn = _interleave([gla, hgrn])
    o_gla = _head_norm_gate(o_gla, params_ref[ROW_GLA_G:ROW_GLA_G + 1, :GLA_DV], proj(OFF_GG, GLA_V),
                            GLA_DV)
    fill(2)
    o_hgrn = _head_norm_gate(o_hgrn, params_ref[ROW_HGRN_G:ROW_HGRN_G + 1, :HGRN_DV],
                             proj(OFF_HG, HGRN_V), HGRN_DV)
    finish()

    mixed = jnp.concatenate([o_gla, o_hgrn], axis=1).astype(BF16)
    return x + gate * jnp.dot(mixed, w_out_ref[...], preferred_element_type=F32)


class _ChannelMixer:
    def __init__(self, x, mod_ref, w1_ref, w2_ref, final_gain):
        self.x, self.w1_ref, self.w2_ref, self.final_gain = x, w1_ref, w2_ref, final_gain
        shift = mod_ref[0, 3:4, :]
        scale = mod_ref[0, 4:5, :]
        self.gate = mod_ref[0, 5:6, :]
        self.h_b = (_rms(x) * (1.0 + scale) + shift).astype(BF16)
        self.acc = None
        self.hidden = []
        self.credit = 0.0
        self.up = 0
        self.down = 0

    def _up(self):
        lo = self.up * FF_BLOCK
        a = jnp.maximum(jnp.dot(self.h_b, self.w1_ref[:, lo:lo + FF_BLOCK],
                                preferred_element_type=F32), 0.0)
        self.hidden.append((a * a).astype(BF16))
        self.up += 1

    def _down(self):
        lo = self.down * FF_BLOCK
        y = jnp.dot(self.hidden.pop(0), self.w2_ref[lo:lo + FF_BLOCK, :],
                    preferred_element_type=F32)
        self.acc = y if self.acc is None else self.acc + y
        self.down += 1

    def step(self):
        n_blocks = D_FF // FF_BLOCK
        if self.up < n_blocks and self.up <= self.down + 1:
            self._up()
        elif self.down < n_blocks:
            self._down()

    def fill(self, n):
        self.credit += n * (2 * D_FF // FF_BLOCK) / FILL_UNITS_PER_MIX
        while self.credit >= 1.0:
            self.step()
            self.credit -= 1.0

    def finish(self):
        while self.down * FF_BLOCK < D_FF:
            self.step()
        return _rms(self.x + self.gate * self.acc) * self.final_gain


def _layer_kernel(tiles_per_seq, x_ref, mod_a_ref, mod_b_ref, w_in_a_ref, w_in_b_ref, w2_ref,
                  params_ref, w_out_ref, w_mlp1_hbm, w_mlp2_hbm, o_ref, s_gla_ref, s_hgrn_ref, x1_ref,
                  w_mlp1_ref, w_mlp2_ref, stage1_ref, stage2_ref, dma_sem):
    i = pl.program_id(0)

    def load_cast(src_hbm, dst_ref, stage_ref):
        rows = stage_ref.shape[1]
        copies = [pltpu.make_async_copy(src_hbm.at[pl.ds(k * rows, rows), :], stage_ref.at[k % 2],
                                        dma_sem.at[k % 2])
                  for k in range(src_hbm.shape[0] // rows)]
        copies[0].start()
        for k, copy in enumerate(copies):
            if k + 1 < len(copies):
                copies[k + 1].start()
            copy.wait()
            dst_ref[pl.ds(k * rows, rows), :] = stage_ref[k % 2].astype(BF16)

    @pl.when(i == 0)
    def _():
        x1_ref[...] = jnp.zeros_like(x1_ref)
        load_cast(w_mlp1_hbm, w_mlp1_ref, stage1_ref)
        load_cast(w_mlp2_hbm, w_mlp2_ref, stage2_ref)

    @pl.when(i % tiles_per_seq == 0)
    def _():
        s_gla_ref[...] = jnp.zeros_like(s_gla_ref)
        s_hgrn_ref[...] = jnp.zeros_like(s_hgrn_ref)

    channel_mixer = []

    def start_filler():
        channel_mixer.append(_ChannelMixer(x1_ref[(i + 1) % 2], mod_b_ref, w_mlp1_ref, w_mlp2_ref,
                                           params_ref[ROW_FINAL_G:ROW_FINAL_G + 1, :]))

    def fill(n):
        channel_mixer[0].fill(n)

    def finish():
        o_ref[0] = channel_mixer[0].finish()

    x1_ref[i % 2] = _token_mixer(
        x_ref[0], mod_a_ref, (w_in_a_ref, w_in_b_ref), w2_ref, params_ref, w_out_ref, s_gla_ref,
        s_hgrn_ref, start_filler, fill, finish)


def _const_spec(shape):
    zeros = (0,) * len(shape)
    return pl.BlockSpec(shape, lambda *_: zeros, pipeline_mode=pl.Buffered(1))


def _nbytes(a):
    return a.size * a.dtype.itemsize


def _layer_vmem_limit(resident, scratch_bytes, tile_bytes):
    intermediates = 3 * TILE * IN_WIDTH_P * 4
    return sum(_nbytes(a) for a in resident) + scratch_bytes + 6 * tile_bytes + intermediates


def _layer(x, mod, w_in_parts, w2_p, params, w_out_b, w_mlp1, w_mlp2):
    batch, seq, _ = x.shape
    tps = seq // TILE
    n_tiles = batch * tps

    def tile_in(i):
        j = jnp.minimum(i, n_tiles - 1)
        return (j // tps, j % tps, 0)

    def tile_out(i):
        j = jnp.maximum(i - 1, 0)
        return (j // tps, j % tps, 0)

    resident = (*w_in_parts, w2_p, params, w_out_b)
    stage1 = (2, CAST_CHUNK_BYTES // (w_mlp1.shape[1] * 4), w_mlp1.shape[1])
    stage2 = (2, CAST_CHUNK_BYTES // (w_mlp2.shape[1] * 4), w_mlp2.shape[1])
    weight_scratch_bytes = (w_mlp1.size + w_mlp2.size) * 2 + 4 * CAST_CHUNK_BYTES
    return pl.pallas_call(
        functools.partial(_layer_kernel, tps),
        grid=(n_tiles + 1,),
        in_specs=[
            pl.BlockSpec((1, TILE, D_MODEL), tile_in),
            pl.BlockSpec((1, N_MOD, D_MODEL), lambda i: (tile_in(i)[0], 0, 0)),
            pl.BlockSpec((1, N_MOD, D_MODEL), lambda i: (tile_out(i)[0], 0, 0)),
            *[_const_spec(a.shape) for a in resident],
            pl.BlockSpec(memory_space=pl.ANY),
            pl.BlockSpec(memory_space=pl.ANY),
        ],
        out_specs=pl.BlockSpec((1, TILE, D_MODEL), tile_out),
        out_shape=jax.ShapeDtypeStruct(x.shape, F32),
        scratch_shapes=[
            pltpu.VMEM((GLA_HEADS, GLA_DK, GLA_DV), F32),
            pltpu.VMEM((HGRN_HEADS, HGRN_DK, HGRN_DV), F32),
            pltpu.VMEM((2, TILE, D_MODEL), F32),
            pltpu.VMEM(w_mlp1.shape, BF16),
            pltpu.VMEM(w_mlp2.shape, BF16),
            pltpu.VMEM(stage1, F32),
            pltpu.VMEM(stage2, F32),
            pltpu.SemaphoreType.DMA((2,)),
        ],
        compiler_params=pltpu.CompilerParams(
            dimension_semantics=("arbitrary",),
            vmem_limit_bytes=_layer_vmem_limit(resident, weight_scratch_bytes,
                                               TILE * D_MODEL * 4)),
        name="hybrid_layer",
    )(x, mod, mod, *resident, w_mlp1, w_mlp2)


def _split_w_in(w_in):
    lr_hi = 2 * GLA_QK + GLA_V + GLA_GATE_RANK
    pad = jnp.zeros((w_in.shape[0], RANK_PAD - GLA_GATE_RANK), w_in.dtype)
    return (jnp.concatenate([w_in[:, :lr_hi], pad], axis=1).astype(BF16),
            w_in[:, lr_hi:].astype(BF16))


def kernel(x, c, w_ada, b_ada, w_in, gla_gate_w2, gla_gate_b2, gla_norm_g, hgrn_lb_logits,
           hgrn_norm_g, w_out, w_mlp1, w_mlp2, final_norm_g):
    assert w_ada.shape[0] == 1, "single-layer trunk"
    assert x.shape[1] % TILE == 0 and x.shape[2] == D_MODEL
    batch = x.shape[0]
    mod = _modulation(c, w_ada[0], b_ada[0]).reshape(batch, N_MOD, D_MODEL)

    w2_p = jnp.concatenate(
        [gla_gate_w2[0], jnp.zeros((RANK_PAD - GLA_GATE_RANK, GLA_QK), F32)], axis=0).astype(BF16)
    def row(v):
        return jnp.pad(v.reshape(1, -1), ((0, 0), (0, D_MODEL - v.size)))

    params = jnp.concatenate(
        [row(gla_gate_b2[0]), row(gla_norm_g[0]), row(hgrn_norm_g[0]), row(final_norm_g),
         jnp.pad(hgrn_lb_logits, ((0, 0), (0, D_MODEL - HGRN_K))),
         jnp.zeros((SUBLANES - ROW_LB - 2, D_MODEL), F32)], axis=0)
    return _layer(x, mod, _split_w_in(w_in[0]), w2_p, params, w_out[0].astype(BF16),
                  w_mlp1[0], w_mlp2[0])
```

```python
import functools

import jax
import jax.numpy as jnp
from jax import lax
from jax.experimental import pallas as pl
from jax.experimental.pallas import tpu as pltpu

F32 = jnp.float32
BF16 = jnp.bfloat16

D_MODEL = 1024
GLA_HEADS = 4
GLA_DK = 64
GLA_DV = 128
GLA_QK = GLA_HEADS * GLA_DK
GLA_V = GLA_HEADS * GLA_DV
GLA_GATE_RANK = 16
GLA_GATE_NORMALIZER = 16.0
HGRN_HEADS = 4
HGRN_DK = 128
HGRN_DV = 128
HGRN_K = HGRN_HEADS * HGRN_DK
HGRN_V = HGRN_HEADS * HGRN_DV
CHUNK = 64
D_FF = 4 * D_MODEL
N_MOD = 6
EPS = 1e-6

LANES = 128
SUBLANES = 8
RANK_PAD = LANES

OFF_GQ = 0
OFF_GK = OFF_GQ + GLA_QK
OFF_GV = OFF_GK + GLA_QK
OFF_LR = OFF_GV + GLA_V
OFF_GG = OFF_LR + RANK_PAD
OFF_HQ = OFF_GG + GLA_V
OFF_HF = OFF_HQ + HGRN_K
OFF_HI = OFF_HF + HGRN_K
OFF_HG = OFF_HI + HGRN_V
IN_WIDTH_P = OFF_HG + HGRN_V

TILE = 256
FF_BLOCK = 256
GROUP_FILL = (2, 1, 2)
FILL_UNITS_PER_MIX = 30
ROW_B2, ROW_GLA_G, ROW_HGRN_G, ROW_FINAL_G, ROW_LB = 0, 1, 2, 3, 4
CAST_CHUNK_BYTES = 2 * 1024 * 1024
HEADS_PER_DOT = 2


def _rms(x):
    return x * lax.rsqrt(jnp.mean(x * x, axis=-1, keepdims=True) + EPS)


def _sigmoid(x):
    return 1.0 / (1.0 + jnp.exp(-x))


def _silu(x):
    return x * _sigmoid(x)


def _log_sigmoid(x):
    return jnp.minimum(x, 0.0) - jnp.log(1.0 + jnp.exp(-jnp.abs(x)))


def _mod_kernel(c_ref, w_ref, b_ref, o_ref):
    cond = _silu(c_ref[...])
    o_ref[...] = jnp.dot(cond.astype(BF16), w_ref[...].astype(BF16),
                         preferred_element_type=F32) + b_ref[...]


def _modulation(c, w_ada, b_ada):
    batch = c.shape[0]
    n_out = w_ada.shape[1]
    blk = D_MODEL
    return pl.pallas_call(
        _mod_kernel,
        grid=(n_out // blk,),
        in_specs=[
            pl.BlockSpec((batch, D_MODEL), lambda j: (0, 0)),
            pl.BlockSpec((D_MODEL, blk), lambda j: (0, j)),
            pl.BlockSpec((1, blk), lambda j: (0, j)),
        ],
        out_specs=pl.BlockSpec((batch, blk), lambda j: (0, j)),
        out_shape=jax.ShapeDtypeStruct((batch, n_out), F32),
        name="adaln_mod",
    )(c, w_ada, b_ada.reshape(1, n_out))


def _chunk_scan(x):
    row = lax.broadcasted_iota(jnp.int32, (SUBLANES, x.shape[1]), 0)
    out = []
    carry = None
    for g in range(x.shape[0] // SUBLANES):
        p = x[g * SUBLANES:(g + 1) * SUBLANES, :]
        s = 1
        while s < SUBLANES:
            p = p + jnp.where(row >= s, pltpu.roll(p, s, 0), 0.0)
            s *= 2
        if carry is not None:
            p = p + carry
        carry = p[SUBLANES - 1:SUBLANES, :]
        out.append(p)
    return jnp.concatenate(out, axis=0)


def _block_diag(blocks, n):
    zero = jnp.zeros_like(blocks[0])
    return jnp.concatenate(
        [jnp.concatenate([blocks[h] if g == h else zero for g in range(n)], axis=1)
         for h in range(n)], axis=0)


def _gated_linear_attention(q_fn, k_la_fn, v, s_ref, n_heads, dk, dv, fill):
    tile = v.shape[0]
    n_chunks = tile // CHUNK
    width = n_heads * dk
    n_lane_blocks = width // LANES
    fill_pre, fill_per_lane_block, fill_per_chunk = GROUP_FILL

    fill(fill_pre)
    q_rel = [[None] * n_lane_blocks for _ in range(n_chunks)]
    k_rel = [[None] * n_lane_blocks for _ in range(n_chunks)]
    k_end = [[None] * n_lane_blocks for _ in range(n_chunks)]
    q_in = [[None] * n_lane_blocks for _ in range(n_chunks)]
    last = [[None] * n_lane_blocks for _ in range(n_chunks)]
    for c in range(n_lane_blocks):
        cols = slice(c * LANES, (c + 1) * LANES)
        for j in range(n_chunks):
            rows = slice(j * CHUNK, (j + 1) * CHUNK)
            q = q_fn(rows, cols)
            k, log_a = k_la_fn(rows, cols)
            cum = _chunk_scan(log_a)
            cum_mid = cum[CHUNK // 2 - 1:CHUNK // 2, :]
            cum_last = cum[CHUNK - 1:CHUNK, :]
            q_rel[j][c] = (q * jnp.exp(cum - cum_mid)).astype(BF16)
            k_rel[j][c] = k * jnp.exp(cum_mid - cum)
            k_end[j][c] = k * jnp.exp(cum_last - cum)
            q_in[j][c] = (q * jnp.exp(cum)).astype(BF16)
            last[j][c] = cum_last
        fill(fill_per_lane_block)

    chunk_row = lax.broadcasted_iota(jnp.int32, (SUBLANES, width), 0)
    last_rows = jnp.zeros((SUBLANES, width), F32)
    for j in range(n_chunks):
        last_rows = jnp.where(chunk_row == j, jnp.concatenate(last[j], axis=1), last_rows)
    last_pad = jnp.concatenate([last_rows, jnp.zeros((LANES - SUBLANES, width), F32)], axis=0)
    decay_t = jnp.exp(last_pad.T)

    yield
    v_b = v.astype(BF16)
    hpd = HEADS_PER_DOT
    n_groups = n_heads // hpd
    gw = hpd * dk
    key_le_query = (lax.broadcasted_iota(jnp.int32, (CHUNK, hpd * CHUNK), 1) % CHUNK
                    <= lax.broadcasted_iota(jnp.int32, (CHUNK, hpd * CHUNK), 0))
    head_of_lane = lax.broadcasted_iota(jnp.int32, (CHUNK, gw), 1) // dk

    def group_scores(j, g):
        k_g = jnp.concatenate(k_rel[j], axis=1)[:, g * gw:(g + 1) * gw]
        if dk % LANES == 0:
            k_bd = _block_diag([k_g[:, b * dk:(b + 1) * dk] for b in range(hpd)], hpd)
        else:
            k_bd = jnp.concatenate(
                [jnp.where(head_of_lane == b, k_g, 0.0) for b in range(hpd)], axis=0)
        q_g = jnp.concatenate(q_rel[j], axis=1)[:, g * gw:(g + 1) * gw]
        return jnp.dot(q_g, k_bd.T.astype(BF16), preferred_element_type=F32)

    state = [s_ref[h] for h in range(n_heads)]
    out = []
    scores_next = [group_scores(0, g) for g in range(n_groups)]
    for j in range(n_chunks):
        rows = slice(j * CHUNK, (j + 1) * CHUNK)
        scores = scores_next
        if j + 1 < n_chunks:
            scores_next = [group_scores(j + 1, g) for g in range(n_groups)]
        fill(fill_per_chunk)
        q_in_j = jnp.concatenate(q_in[j], axis=1)
        out_j = []
        new_state = list(state)
        for g in range(n_groups):
            heads = range(g * hpd, (g + 1) * hpd)
            v_heads = [v_b[rows, h * dv:(h + 1) * dv] for h in heads]
            v_bd = _block_diag(v_heads, hpd)
            p = jnp.where(key_le_query, scores[g], 0.0).astype(BF16)
            w_state = _block_diag([state[h].astype(BF16) for h in heads], hpd)
            o_g = jnp.dot(jnp.concatenate([p, q_in_j[:, g * gw:(g + 1) * gw]], axis=1),
                          jnp.concatenate([v_bd, w_state], axis=0), preferred_element_type=F32)
            out_j.append(o_g)
            if dk % LANES == 0:
                k_t = jnp.concatenate([k_end[j][h] for h in heads], axis=0).T.astype(BF16)
                upd = jnp.dot(k_t, v_bd, preferred_element_type=F32)
                upd_heads = [upd[:, b * dv:(b + 1) * dv] for b in range(hpd)]
            else:
                k_t = k_end[j][g].T.astype(BF16)
                upd = jnp.dot(k_t, jnp.concatenate(v_heads, axis=1),
                              preferred_element_type=F32)
                upd_heads = [upd[b * dk:(b + 1) * dk, b * dv:(b + 1) * dv] for b in range(hpd)]
            for b, h in enumerate(heads):
                new_state[h] = decay_t[h * dk:(h + 1) * dk, j:j + 1] * state[h] + upd_heads[b]
        state = new_state
        out.append(jnp.concatenate(out_j, axis=1))
        yield
    for h in range(n_heads):
        s_ref[h] = state[h]
    out = jnp.concatenate(out, axis=0)
    return [out[:, h * dv:(h + 1) * dv] for h in range(n_heads)]


def _interleave(generators):
    results = [None] * len(generators)
    live = list(range(len(generators)))
    while live:
        for idx in list(live):
            try:
                next(generators[idx])
            except StopIteration as done:
                results[idx] = done.value
                live.remove(idx)
    return results


def _head_norm_gate(heads, gain, gate, dv):
    outs = [_rms(o_h) * gain * _silu(gate[:, h * dv:(h + 1) * dv]) for h, o_h in enumerate(heads)]
    return jnp.concatenate(outs, axis=1)


def _token_mixer(x, mod_ref, w_in_refs, w2_ref, params_ref, w_out_ref, s_gla_ref, s_hgrn_ref,
                 start_filler, fill, finish):
    shift = mod_ref[0, 0:1, :]
    scale = mod_ref[0, 1:2, :]
    gate = mod_ref[0, 2:3, :]
    h_b = (_rms(x) * (1.0 + scale) + shift).astype(BF16)

    pieces = [jnp.dot(h_b, ref[...], preferred_element_type=F32) for ref in w_in_refs]
    start_filler()
    fill(2)

    def proj(off, width):
        for piece in pieces:
            if off < piece.shape[1]:
                return piece[:, off:off + width]
            off -= piece.shape[1]

    gate_logits = jnp.dot(proj(OFF_LR, RANK_PAD).astype(BF16), w2_ref[...],
                          preferred_element_type=F32) + params_ref[ROW_B2:ROW_B2 + 1, :GLA_QK]
    gq = proj(OFF_GQ, GLA_QK)
    gk = proj(OFF_GK, GLA_QK)
    gla = _gated_linear_attention(
        lambda rows, cols: gq[rows, cols] * (GLA_DK ** -0.5),
        lambda rows, cols: (gk[rows, cols],
                            _log_sigmoid(gate_logits[rows, cols]) * (1.0 / GLA_GATE_NORMALIZER)),
        proj(OFF_GV, GLA_V), s_gla_ref, GLA_HEADS, GLA_DK, GLA_DV, fill)

    lb_logits = params_ref[ROW_LB:ROW_LB + 2, :HGRN_K]
    lb_e = jnp.exp(lb_logits - jnp.max(lb_logits, axis=0, keepdims=True))
    lb = lb_e[0:1, :] / jnp.sum(lb_e, axis=0, keepdims=True)
    hq = proj(OFF_HQ, HGRN_K)
    hf = proj(OFF_HF, HGRN_K)

    def hgrn_k_la(rows, cols):
        forget = lb[:, cols] + (1.0 - lb[:, cols]) * _sigmoid(hf[rows, cols])
        return 1.0 - forget, jnp.log(forget)

    hgrn = _gated_linear_attention(
        lambda rows, cols: _silu(hq[rows, cols]), hgrn_k_la, proj(OFF_HI, HGRN_V),
        s_hgrn_ref, HGRN_HEADS, HGRN_DK, HGRN_DV, fill)

    o_gla, o_hgrn = _interleave([gla, hgrn])
    o_gla = _head_norm_gate(o_gla, params_ref[ROW_GLA_G:ROW_GLA_G + 1, :GLA_DV], proj(OFF_GG, GLA_V),
                            GLA_DV)
    fill(2)
    o_hgrn = _head_norm_gate(o_hgrn, params_ref[ROW_HGRN_G:ROW_HGRN_G + 1, :HGRN_DV],
                             proj(OFF_HG, HGRN_V), HGRN_DV)
    finish()

    mixed = jnp.concatenate([o_gla, o_hgrn], axis=1).astype(BF16)
    return x + gate * jnp.dot(mixed, w_out_ref[...], preferred_element_type=F32)


class _ChannelMixer:
    def __init__(self, x, mod_ref, w1_ref, w2_ref, final_gain):
        self.x, self.w1_ref, self.w2_ref, self.final_gain = x, w1_ref, w2_ref, final_gain
        shift = mod_ref[0, 3:4, :]
        scale = mod_ref[0, 4:5, :]
        self.gate = mod_ref[0, 5:6, :]
        self.h_b = (_rms(x) * (1.0 + scale) + shift).astype(BF16)
        self.acc = None
        self.hidden = []
        self.credit = 0.0
        self.up = 0
        self.down = 0

    def _up(self):
        lo = self.up * FF_BLOCK
        a = jnp.maximum(jnp.dot(self.h_b, self.w1_ref[:, lo:lo + FF_BLOCK],
                                preferred_element_type=F32), 0.0)
        self.hidden.append((a * a).astype(BF16))
        self.up += 1

    def _down(self):
        lo = self.down * FF_BLOCK
        y = jnp.dot(self.hidden.pop(0), self.w2_ref[lo:lo + FF_BLOCK, :],
                    preferred_element_type=F32)
        self.acc = y if self.acc is None else self.acc + y
        self.down += 1

    def step(self):
        n_blocks = D_FF // FF_BLOCK
        if self.up < n_blocks and self.up <= self.down + 1:
            self._up()
        elif self.down < n_blocks:
            self._down()

    def fill(self, n):
        self.credit += n * (2 * D_FF // FF_BLOCK) / FILL_UNITS_PER_MIX
        while self.credit >= 1.0:
            self.step()
            self.credit -= 1.0

    def finish(self):
        while self.down * FF_BLOCK < D_FF:
            self.step()
        return _rms(self.x + self.gate * self.acc) * self.final_gain


def _layer_kernel(tiles_per_seq, x_ref, mod_a_ref, mod_b_ref, w_in_a_ref, w_in_b_ref, w2_ref,
                  params_ref, w_out_hbm, w_mlp1_hbm, w_mlp2_hbm, o_ref, s_gla_ref, s_hgrn_ref, x1_ref,
                  w_out_ref, w_mlp1_ref, w_mlp2_ref, stage1_ref, stage2_ref, dma_sem):
    i = pl.program_id(0)

    def load_cast(src_hbm, dst_ref, stage_ref):
        rows = stage_ref.shape[1]
        copies = [pltpu.make_async_copy(src_hbm.at[pl.ds(k * rows, rows), :], stage_ref.at[k % 2],
                                        dma_sem.at[k % 2])
                  for k in range(src_hbm.shape[0] // rows)]
        copies[0].start()
        for k, copy in enumerate(copies):
            if k + 1 < len(copies):
                copies[k + 1].start()
            copy.wait()
            dst_ref[pl.ds(k * rows, rows), :] = stage_ref[k % 2].astype(BF16)

    @pl.when(i == 0)
    def _():
        x1_ref[...] = jnp.zeros_like(x1_ref)
        load_cast(w_out_hbm, w_out_ref, stage2_ref)
        load_cast(w_mlp1_hbm, w_mlp1_ref, stage1_ref)
        load_cast(w_mlp2_hbm, w_mlp2_ref, stage2_ref)

    @pl.when(i % tiles_per_seq == 0)
    def _():
        s_gla_ref[...] = jnp.zeros_like(s_gla_ref)
        s_hgrn_ref[...] = jnp.zeros_like(s_hgrn_ref)

    channel_mixer = []

    def start_filler():
        channel_mixer.append(_ChannelMixer(x1_ref[(i + 1) % 2], mod_b_ref, w_mlp1_ref, w_mlp2_ref,
                                           params_ref[ROW_FINAL_G:ROW_FINAL_G + 1, :]))

    def fill(n):
        channel_mixer[0].fill(n)

    def finish():
        o_ref[0] = channel_mixer[0].finish()

    x1_ref[i % 2] = _token_mixer(
        x_ref[0], mod_a_ref, (w_in_a_ref, w_in_b_ref), w2_ref, params_ref, w_out_ref, s_gla_ref,
        s_hgrn_ref, start_filler, fill, finish)


def _const_spec(shape):
    zeros = (0,) * len(shape)
    return pl.BlockSpec(shape, lambda *_: zeros, pipeline_mode=pl.Buffered(1))


def _nbytes(a):
    return a.size * a.dtype.itemsize


def _layer_vmem_limit(resident, scratch_bytes, tile_bytes):
    intermediates = 3 * TILE * IN_WIDTH_P * 4
    return sum(_nbytes(a) for a in resident) + scratch_bytes + 6 * tile_bytes + intermediates


def _layer(x, mod, w_in_parts, w2_p, params, w_out, w_mlp1, w_mlp2):
    batch, seq, _ = x.shape
    tps = seq // TILE
    n_tiles = batch * tps

    def tile_in(i):
        j = jnp.minimum(i, n_tiles - 1)
        return (j // tps, j % tps, 0)

    def tile_out(i):
        j = jnp.maximum(i - 1, 0)
        return (j // tps, j % tps, 0)

    resident = (*w_in_parts, w2_p, params)
    stage1 = (2, CAST_CHUNK_BYTES // (w_mlp1.shape[1] * 4), w_mlp1.shape[1])
    stage2 = (2, CAST_CHUNK_BYTES // (w_mlp2.shape[1] * 4), w_mlp2.shape[1])
    weight_scratch_bytes = (w_out.size + w_mlp1.size + w_mlp2.size) * 2 + 4 * CAST_CHUNK_BYTES
    return pl.pallas_call(
        functools.partial(_layer_kernel, tps),
        grid=(n_tiles + 1,),
        in_specs=[
            pl.BlockSpec((1, TILE, D_MODEL), tile_in),
            pl.BlockSpec((1, N_MOD, D_MODEL), lambda i: (tile_in(i)[0], 0, 0)),
            pl.BlockSpec((1, N_MOD, D_MODEL), lambda i: (tile_out(i)[0], 0, 0)),
            *[_const_spec(a.shape) for a in resident],
            pl.BlockSpec(memory_space=pl.ANY),
            pl.BlockSpec(memory_space=pl.ANY),
            pl.BlockSpec(memory_space=pl.ANY),
        ],
        out_specs=pl.BlockSpec((1, TILE, D_MODEL), tile_out),
        out_shape=jax.ShapeDtypeStruct(x.shape, F32),
        scratch_shapes=[
            pltpu.VMEM((GLA_HEADS, GLA_DK, GLA_DV), F32),
            pltpu.VMEM((HGRN_HEADS, HGRN_DK, HGRN_DV), F32),
            pltpu.VMEM((2, TILE, D_MODEL), F32),
            pltpu.VMEM(w_out.shape, BF16),
            pltpu.VMEM(w_mlp1.shape, BF16),
            pltpu.VMEM(w_mlp2.shape, BF16),
            pltpu.VMEM(stage1, F32),
            pltpu.VMEM(stage2, F32),
            pltpu.SemaphoreType.DMA((2,)),
        ],
        compiler_params=pltpu.CompilerParams(
            dimension_semantics=("arbitrary",),
            vmem_limit_bytes=_layer_vmem_limit(resident, weight_scratch_bytes,
                                               TILE * D_MODEL * 4)),
        name="hybrid_layer",
    )(x, mod, mod, *resident, w_out, w_mlp1, w_mlp2)


def _split_w_in(w_in):
    lr_hi = 2 * GLA_QK + GLA_V + GLA_GATE_RANK
    pad = jnp.zeros((w_in.shape[0], RANK_PAD - GLA_GATE_RANK), w_in.dtype)
    return (jnp.concatenate([w_in[:, :lr_hi], pad], axis=1).astype(BF16),
            w_in[:, lr_hi:].astype(BF16))


def kernel(x, c, w_ada, b_ada, w_in, gla_gate_w2, gla_gate_b2, gla_norm_g, hgrn_lb_logits,
           hgrn_norm_g, w_out, w_mlp1, w_mlp2, final_norm_g):
    assert w_ada.shape[0] == 1, "single-layer trunk"
    assert x.shape[1] % TILE == 0 and x.shape[2] == D_MODEL
    batch = x.shape[0]
    mod = _modulation(c, w_ada[0], b_ada[0]).reshape(batch, N_MOD, D_MODEL)

    w2_p = jnp.concatenate(
        [gla_gate_w2[0], jnp.zeros((RANK_PAD - GLA_GATE_RANK, GLA_QK), F32)], axis=0).astype(BF16)
    def row(v):
        return jnp.pad(v.reshape(1, -1), ((0, 0), (0, D_MODEL - v.size)))

    params = jnp.concatenate(
        [row(gla_gate_b2[0]), row(gla_norm_g[0]), row(hgrn_norm_g[0]), row(final_norm_g),
         jnp.pad(hgrn_lb_logits, ((0, 0), (0, D_MODEL - HGRN_K))),
         jnp.zeros((SUBLANES - ROW_LB - 2, D_MODEL), F32)], axis=0)
    return _layer(x, mod, _split_w_in(w_in[0]), w2_p, params, w_out[0], w_mlp1[0], w_mlp2[0])
```
